```python
import jax, jax.numpy as jnp
from jax import lax
import numpy as np

D_MODEL = 1024
BATCH = 8
SEQ = 4096
DEPTH = 2

N_MIXERS = 2
CONV_WIDTH = 3
N_HEADS = 16
N_KV_HEADS = 4
HEAD_DIM = D_MODEL // N_HEADS
GROUP = N_HEADS // N_KV_HEADS
AXIS_DIM = HEAD_DIM // 2
ROPE_THETA = 10000.0
Q_BLOCK = 128
GRID_W = 64
N_EXPERTS = 16
CAPACITY_FACTOR = 2
D_FF_EXPERT = 2 * D_MODEL
EPS = 1e-6

N_CONV_LAYERS = (DEPTH + 1) // 2
N_ATTN_LAYERS = DEPTH // 2

kernel_name = "hybrid_conv_attn_ec_moe_encoder"


def _rms32(x, g):
    x32 = x.astype(jnp.float32)
    return x32 * lax.rsqrt(jnp.mean(x32 * x32, axis=-1, keepdims=True) + EPS) * g.astype(jnp.float32)


def rmsnorm(x, g):
    return _rms32(x, g).astype(x.dtype)


def short_gated_conv(xn, w_in, w_conv, w_out):
    bch = xn @ w_in
    b, c, h = jnp.split(bch, 3, axis=-1)
    u = c * h
    up = jnp.pad(u, ((0, 0), (1, 1), (0, 0)))
    y = w_conv[0] * up[:, :-2] + w_conv[1] * up[:, 1:-1] + w_conv[2] * up[:, 2:]
    return (b * y) @ w_out


def axial_rope_tables(seq_len):
    rows = seq_len // GRID_W
    row = jnp.repeat(jnp.arange(rows, dtype=jnp.float32), GRID_W)
    col = jnp.tile(jnp.arange(GRID_W, dtype=jnp.float32), rows)
    inv_freq = ROPE_THETA ** (-jnp.arange(0, AXIS_DIM, 2, dtype=jnp.float32) / AXIS_DIM)
    ang_r = row[:, None] * inv_freq[None, :]
    ang_c = col[:, None] * inv_freq[None, :]
    return jnp.cos(ang_r), jnp.sin(ang_r), jnp.cos(ang_c), jnp.sin(ang_c)


def _rotate(x, cos, sin):
    x1, x2 = jnp.split(x, 2, axis=-1)
    cos = cos[None, :, None, :]
    sin = sin[None, :, None, :]
    return jnp.concatenate([x1 * cos - x2 * sin, x2 * cos + x1 * sin], axis=-1)


def apply_axial_rope(x, tables):
    cr, sr, cc, sc = tables
    return jnp.concatenate([_rotate(x[..., :AXIS_DIM], cr, sr),
                            _rotate(x[..., AXIS_DIM:], cc, sc)], axis=-1)


def gqa_axial_attention(xn, w_qkv, g_q, g_k, w_o):
    bsz, seq, _ = xn.shape
    dt = xn.dtype
    qkv = xn @ w_qkv
    nq, nk = N_HEADS * HEAD_DIM, N_KV_HEADS * HEAD_DIM
    q = qkv[..., :nq].reshape(bsz, seq, N_HEADS, HEAD_DIM)
    k = qkv[..., nq:nq + nk].reshape(bsz, seq, N_KV_HEADS, HEAD_DIM)
    v = qkv[..., nq + nk:].reshape(bsz, seq, N_KV_HEADS, HEAD_DIM)
    tables = axial_rope_tables(seq)
    q = (apply_axial_rope(_rms32(q, g_q), tables) * (HEAD_DIM ** -0.5)).astype(dt)
    k = apply_axial_rope(_rms32(k, g_k), tables).astype(dt)
    n_blocks = seq // Q_BLOCK
    qb = q.reshape(bsz, n_blocks, Q_BLOCK, N_KV_HEADS, GROUP, HEAD_DIM).transpose(1, 0, 2, 3, 4, 5)

    def block(q_blk):
        s = jnp.einsum('bqkgd,bskd->bkgqs', q_blk, k).astype(jnp.float32)
        p = jax.nn.softmax(s, axis=-1).astype(v.dtype)
        return jnp.einsum('bkgqs,bskd->bqkgd', p, v)

    o = lax.map(block, qb)
    o = o.transpose(1, 0, 2, 3, 4, 5).reshape(bsz, seq, N_HEADS * HEAD_DIM)
    return o @ w_o


def expert_choice_moe(xn, w_router, w_gate, w_up, w_down):
    bsz, seq, d = xn.shape
    cap = CAPACITY_FACTOR * seq // N_EXPERTS
    logits = jnp.einsum('bsd,de->bse', xn, w_router).astype(jnp.float32)
    aff = jax.nn.softmax(logits, axis=-1)
    gate, idx = lax.top_k(aff.transpose(0, 2, 1), cap)
    xg = jax.vmap(lambda xb, ib: xb[ib])(xn, idx)
    h = jax.nn.silu(jnp.einsum('becd,edf->becf', xg, w_gate)) * jnp.einsum('becd,edf->becf', xg, w_up)
    y = jnp.einsum('becf,efd->becd', h, w_down) * gate[..., None].astype(xn.dtype)
    return jax.vmap(lambda yb, ib: jnp.zeros((seq, d), xn.dtype).at[ib.reshape(-1)].add(yb.reshape(-1, d)))(y, idx)


def setup_inputs(seed: int = 0) -> dict:
    key = jax.random.key(seed)
    ks = jax.random.split(key, 16)
    D, E, F = D_MODEL, N_EXPERTS, D_FF_EXPERT
    nqkv = (N_HEADS + 2 * N_KV_HEADS) * HEAD_DIM
    nrm = lambda k, shape, s: jax.random.normal(k, shape, jnp.float32) * s
    return {
        "x": nrm(ks[0], (BATCH, SEQ, D), 1.0),
        "norm_mix": 1.0 + nrm(ks[1], (DEPTH, D), 0.02),
        "norm_ffn": 1.0 + nrm(ks[2], (DEPTH, D), 0.02),
        "conv_in": nrm(ks[3], (N_CONV_LAYERS, D, 3 * D), D ** -0.5),
        "conv_w": nrm(ks[4], (N_CONV_LAYERS, CONV_WIDTH, D), CONV_WIDTH ** -0.5),
        "conv_out": nrm(ks[5], (N_CONV_LAYERS, D, D), D ** -0.5),
        "attn_qkv": nrm(ks[6], (N_ATTN_LAYERS, D, nqkv), D ** -0.5),
        "attn_q_norm": 1.0 + nrm(ks[7], (N_ATTN_LAYERS, HEAD_DIM), 0.02),
        "attn_k_norm": 1.0 + nrm(ks[8], (N_ATTN_LAYERS, HEAD_DIM), 0.02),
        "attn_out": nrm(ks[9], (N_ATTN_LAYERS, N_HEADS * HEAD_DIM, D), D ** -0.5),
        "router": nrm(ks[10], (DEPTH, D, E), D ** -0.5),
        "w_gate": nrm(ks[11], (DEPTH, E, D, F), D ** -0.5),
        "w_up": nrm(ks[12], (DEPTH, E, D, F), D ** -0.5),
        "w_down": nrm(ks[13], (DEPTH, E, F, D), F ** -0.5),
        "final_norm": 1.0 + nrm(ks[14], (D,), 0.02),
    }


def reference(x, norm_mix, norm_ffn, conv_in, conv_w, conv_out, attn_qkv, attn_q_norm,
              attn_k_norm, attn_out, router, w_gate, w_up, w_down, final_norm):
    h = x
    for i in range(DEPTH):
        xn = rmsnorm(h, norm_mix[i])
        j = i // N_MIXERS
        if i % N_MIXERS == 0:
            h = h + short_gated_conv(xn, conv_in[j], conv_w[j], conv_out[j])
        else:
            h = h + gqa_axial_attention(xn, attn_qkv[j], attn_q_norm[j], attn_k_norm[j], attn_out[j])
        xn = rmsnorm(h, norm_ffn[i])
        h = h + expert_choice_moe(xn, router[i], w_gate[i], w_up[i], w_down[i])
    return rmsnorm(h, final_norm)
```

```python
import functools

import jax
import jax.numpy as jnp
import numpy as np
from jax import lax
from jax.experimental import pallas as pl
from jax.experimental.pallas import tpu as pltpu

EPS = 1e-6
N_HEADS = 16
N_KV_HEADS = 4
GRID_W = 64
ROPE_THETA = 10000.0
CAPACITY_FACTOR = 2

LANES = 128
CUMSUM_CHUNK = 256
VMEM_LIMIT = 56 * 1024 * 1024

F32 = jnp.float32
BF16 = jnp.bfloat16


def _cparams(*sem):
    return pltpu.CompilerParams(dimension_semantics=sem, vmem_limit_bytes=VMEM_LIMIT)


def _rms(x, g):
    return x * lax.rsqrt(jnp.mean(x * x, axis=-1, keepdims=True) + EPS) * g


def _dot(a, b):
    return jnp.dot(a, b, preferred_element_type=F32)


def _dot_nt(a, b, precision=None):
    return lax.dot_general(a, b, (((1,), (1,)), ((), ())), precision=precision,
                           preferred_element_type=F32)


HALO = 8


def _conv_kernel(xp_ref, x_ref, xq_ref, g_ref, win_ref, wc_ref, wout_ref, o_ref, u_ref):
    i = pl.program_id(1)
    n = pl.num_programs(1)
    t = x_ref.shape[0]
    d = x_ref.shape[1]
    x = x_ref[...]
    g = g_ref[...]
    xs = jnp.concatenate([_rms(xp_ref[...], g), _rms(x, g), _rms(xq_ref[...], g)], axis=0).astype(BF16)
    c = _dot(xs, win_ref[:, d:2 * d])
    h = _dot(xs, win_ref[:, 2 * d:3 * d])
    u = c * h
    row = lax.broadcasted_iota(jnp.int32, (t + 2 * HALO, 1), 0)
    first_live = jnp.where(i == 0, HALO, 0)
    end_live = jnp.where(i == n - 1, HALO + t, t + 2 * HALO)
    u_ref[...] = jnp.where((row >= first_live) & (row < end_live), u, 0.0)
    wc = wc_ref[...]
    y = (wc[0:1, :] * u_ref[HALO - 1:HALO - 1 + t, :] + wc[1:2, :] * u_ref[HALO:HALO + t, :]
         + wc[2:3, :] * u_ref[HALO + 1:HALO + 1 + t, :])
    b = _dot(xs, win_ref[:, 0:d])[HALO:HALO + t, :]
    o_ref[...] = x + _dot((b * y).astype(BF16), wout_ref[...])


def _conv_layer(h, g, w_in, w_conv, w_out, *, tile):
    bsz, seq, d = h.shape
    tb = tile // HALO
    nb = seq // HALO
    return pl.pallas_call(
        _conv_kernel,
        grid=(bsz, seq // tile),
        in_specs=[
            pl.BlockSpec((None, HALO, d), lambda b, i: (b, jnp.maximum(i * tb - 1, 0), 0)),
            pl.BlockSpec((None, tile, d), lambda b, i: (b, i, 0)),
            pl.BlockSpec((None, HALO, d), lambda b, i: (b, jnp.minimum((i + 1) * tb, nb - 1), 0)),
            pl.BlockSpec((1, d), lambda b, i: (0, 0)),
            pl.BlockSpec((d, 3 * d), lambda b, i: (0, 0)),
            pl.BlockSpec((3, d), lambda b, i: (0, 0)),
            pl.BlockSpec((d, d), lambda b, i: (0, 0)),
        ],
        out_specs=pl.BlockSpec((None, tile, d), lambda b, i: (b, i, 0)),
        out_shape=jax.ShapeDtypeStruct((bsz, seq, d), F32),
        scratch_shapes=[pltpu.VMEM((tile + 2 * HALO, d), F32)],
        compiler_params=_cparams("parallel", "arbitrary"),
        name="conv_mixer",
    )(h, h, h, g.reshape(1, d), w_in.astype(BF16), w_conv, w_out.astype(BF16))


def _rope_tables(seq, head_dim, g, scale):
    axis = head_dim // 2
    quarter = axis // 2
    rows = seq // GRID_W
    row = jnp.repeat(jnp.arange(rows, dtype=F32), GRID_W)
    col = jnp.tile(jnp.arange(GRID_W, dtype=F32), rows)
    inv_freq = ROPE_THETA ** (-jnp.arange(0, axis, 2, dtype=F32) / axis)
    j = np.arange(head_dim)
    pos = jnp.where((j < axis)[None, :], row[:, None], col[:, None])
    ang = pos * inv_freq[j % quarter][None, :]
    first = (j % axis) < quarter
    partner = np.where(first, j + quarter, j - quarter)
    cos_t = jnp.cos(ang) * g[None, :] * scale
    sin_t = jnp.sin(ang) * jnp.where(first, -1.0, 1.0)[None, :] * g[partner][None, :] * scale
    return jnp.tile(cos_t, (1, 2)), jnp.tile(sin_t, (1, 2)), partner


def _qkv_kernel(x_ref, g_ref, w_ref, cq_ref, sq_ref, ck_ref, sk_ref, q_ref, k_ref, v_ref, *, hd):
    t, d = x_ref.shape
    nq = q_ref.shape[1]
    nk = k_ref.shape[1] // 2
    xn = _rms(x_ref[...], g_ref[...]).astype(BF16)
    lane = lax.broadcasted_iota(jnp.int32, (1, LANES), 1)
    lo = lane < hd

    def head_scale(raw):
        sq = raw * raw
        s0 = jnp.sum(jnp.where(lo, sq, 0.0), axis=-1, keepdims=True)
        s1 = jnp.sum(jnp.where(lo, 0.0, sq), axis=-1, keepdims=True)
        return jnp.where(lo, lax.rsqrt(s0 / hd + EPS), lax.rsqrt(s1 / hd + EPS))

    def dup(pair):
        sw = pltpu.roll(pair, hd, axis=1)
        return jnp.where(lo, pair, sw), jnp.where(lo, sw, pair)

    cq, sq_t = cq_ref[...], sq_ref[...]
    for j in range(nq // LANES):
        raw = _dot(xn, w_ref[:, j * LANES:(j + 1) * LANES])
        swp = _dot(xn, w_ref[:, nq + j * LANES:nq + (j + 1) * LANES])
        q_ref[:, j * LANES:(j + 1) * LANES] = ((raw * cq + swp * sq_t) * head_scale(raw)).astype(BF16)
    ck, sk_t = ck_ref[...], sk_ref[...]
    for j in range(nk // LANES):
        raw = _dot(xn, w_ref[:, 2 * nq + j * LANES:2 * nq + (j + 1) * LANES])
        swp = _dot(xn, w_ref[:, 2 * nq + nk + j * LANES:2 * nq + nk + (j + 1) * LANES])
        ka, kb = dup(((raw * ck + swp * sk_t) * head_scale(raw)).astype(F32))
        k_ref[:, 2 * j * LANES:(2 * j + 1) * LANES] = ka.astype(BF16)
        k_ref[:, (2 * j + 1) * LANES:(2 * j + 2) * LANES] = kb.astype(BF16)
        va, vb = dup(_dot(xn, w_ref[:, 2 * nq + 2 * nk + j * LANES:2 * nq + 2 * nk + (j + 1) * LANES]))
        v_ref[:, 2 * j * LANES:(2 * j + 1) * LANES] = va.astype(BF16)
        v_ref[:, (2 * j + 1) * LANES:(2 * j + 2) * LANES] = vb.astype(BF16)


def _qkv_layer(h, g, w_qkv, g_q, g_k, *, tile):
    bsz, seq, d = h.shape
    hd = g_q.shape[0]
    assert 2 * hd == LANES
    nq, nk = N_HEADS * hd, N_KV_HEADS * hd
    cq, sq, partner = _rope_tables(seq, hd, g_q, hd ** -0.5)
    ck, sk, _ = _rope_tables(seq, hd, g_k, 1.0)
    wq, wk, wv = w_qkv[:, :nq], w_qkv[:, nq:nq + nk], w_qkv[:, nq + nk:]
    perm_q = (np.arange(nq) // hd) * hd + partner[np.arange(nq) % hd]
    perm_k = (np.arange(nk) // hd) * hd + partner[np.arange(nk) % hd]
    w_all = jnp.concatenate([wq, wq[:, perm_q], wk, wk[:, perm_k], wv], axis=1).astype(BF16)
    ncol = w_all.shape[1]
    tab = pl.BlockSpec((tile, LANES), lambda b, i: (i, 0))
    return pl.pallas_call(
        functools.partial(_qkv_kernel, hd=hd),
        grid=(bsz, seq // tile),
        in_specs=[
            pl.BlockSpec((None, tile, d), lambda b, i: (b, i, 0)),
            pl.BlockSpec((1, d), lambda b, i: (0, 0)),
            pl.BlockSpec((d, ncol), lambda b, i: (0, 0)),
            tab, tab, tab, tab,
        ],
        out_specs=[
            pl.BlockSpec((None, tile, nq), lambda b, i: (b, i, 0)),
            pl.BlockSpec((None, tile, 2 * nk), lambda b, i: (b, i, 0)),
            pl.BlockSpec((None, tile, 2 * nk), lambda b, i: (b, i, 0)),
        ],
        out_shape=[
            jax.ShapeDtypeStruct((bsz, seq, nq), BF16),
            jax.ShapeDtypeStruct((bsz, seq, 2 * nk), BF16),
            jax.ShapeDtypeStruct((bsz, seq, 2 * nk), BF16),
        ],
        compiler_params=_cparams("parallel", "arbitrary"),
        name="qkv_rope",
    )(h, g.reshape(1, d), w_all, cq, sq, ck, sk)


def _attn_kernel(q_ref, k_ref, v_ref, o_ref, m_ref, l_ref, acc_ref, *, hd):
    ki = pl.program_id(3)
    nh = q_ref.shape[1] // hd
    tq = q_ref.shape[0]

    @pl.when(ki == 0)
    def _():
        m_ref[...] = jnp.full(m_ref.shape, -jnp.inf, F32)
        l_ref[...] = jnp.zeros(l_ref.shape, F32)
        acc_ref[...] = jnp.zeros(acc_ref.shape, F32)

    lane = lax.broadcasted_iota(jnp.int32, (1, LANES), 1)
    k2 = k_ref[...]
    v2 = v_ref[...]
    zero = jnp.zeros((), BF16)
    for h in range(nh):
        pair, half = h // 2, h % 2
        mine = (lane < hd) if half == 0 else (lane >= hd)
        qm = jnp.where(mine, q_ref[:, pair * LANES:(pair + 1) * LANES], zero)
        s = _dot_nt(qm, k2)
        m_old = m_ref[h]
        m_new = jnp.maximum(m_old, jnp.max(s, axis=-1, keepdims=True))
        alpha = jnp.exp(m_old - m_new)
        p = jnp.exp(s - m_new[:, 0:1])
        l_ref[h] = alpha * l_ref[h] + jnp.sum(p, axis=-1, keepdims=True)
        m_ref[h] = m_new
        pv = _dot(p.astype(BF16), jnp.where(mine, v2, zero))
        acc_ref[pair] = acc_ref[pair] * jnp.where(mine, alpha, 1.0) + pv

    @pl.when(ki == pl.num_programs(3) - 1)
    def _():
        for pair in range(nh // 2):
            l = jnp.where(lane < hd, l_ref[2 * pair], l_ref[2 * pair + 1])
            o_ref[:, pair * LANES:(pair + 1) * LANES] = (acc_ref[pair] / l).astype(o_ref.dtype)


def _attention(q, k2, v2, *, hd, tq, tk):
    bsz, seq, nq = q.shape
    group = N_HEADS // N_KV_HEADS
    gw = group * hd
    return pl.pallas_call(
        functools.partial(_attn_kernel, hd=hd),
        grid=(bsz, N_KV_HEADS, seq // tq, seq // tk),
        in_specs=[
            pl.BlockSpec((None, tq, gw), lambda b, g, qi, ki: (b, qi, g)),
            pl.BlockSpec((None, tk, LANES), lambda b, g, qi, ki: (b, ki, g)),
            pl.BlockSpec((None, tk, LANES), lambda b, g, qi, ki: (b, ki, g)),
        ],
        out_specs=pl.BlockSpec((None, tq, gw), lambda b, g, qi, ki: (b, qi, g)),
        out_shape=jax.ShapeDtypeStruct((bsz, seq, nq), BF16),
        scratch_shapes=[
            pltpu.VMEM((group, tq, LANES), F32),
            pltpu.VMEM((group, tq, LANES), F32),
            pltpu.VMEM((group // 2, tq, LANES), F32),
        ],
        compiler_params=_cparams("parallel", "parallel", "parallel", "arbitrary"),
        name="flash_attn",
    )(q, k2, v2)


def _oproj_kernel(o_ref, h_ref, w_ref, out_ref):
    out_ref[...] = h_ref[...] + _dot(o_ref[...], w_ref[...])


def _oproj_layer(o, h, w_o, *, tile):
    bsz, seq, d = h.shape
    nq = o.shape[2]
    return pl.pallas_call(
        _oproj_kernel,
        grid=(bsz, seq // tile),
        in_specs=[
            pl.BlockSpec((None, tile, nq), lambda b, i: (b, i, 0)),
            pl.BlockSpec((None, tile, d), lambda b, i: (b, i, 0)),
            pl.BlockSpec((nq, d), lambda b, i: (0, 0)),
        ],
        out_specs=pl.BlockSpec((None, tile, d), lambda b, i: (b, i, 0)),
        out_shape=jax.ShapeDtypeStruct((bsz, seq, d), F32),
        compiler_params=_cparams("parallel", "arbitrary"),
        name="attn_out_proj",
    )(o, h, w_o.astype(BF16))


def _router_kernel(h_ref, g_ref, wt_ref, aff_ref):
    xn = _rms(h_ref[...], g_ref[...])
    lg = _dot_nt(wt_ref[...], xn, precision=lax.Precision.HIGHEST)
    ex = jnp.exp(lg - jnp.max(lg, axis=0, keepdims=True))
    aff_ref[...] = ex / jnp.sum(ex, axis=0, keepdims=True)


def _router(h, g, w_router, *, tile):
    bsz, seq, d = h.shape
    ne = w_router.shape[1]
    return pl.pallas_call(
        _router_kernel,
        grid=(bsz, seq // tile),
        in_specs=[
            pl.BlockSpec((None, tile, d), lambda b, i: (b, i, 0)),
            pl.BlockSpec((1, d), lambda b, i: (0, 0)),
            pl.BlockSpec((ne, d), lambda b, i: (0, 0)),
        ],
        out_specs=pl.BlockSpec((None, ne, tile), lambda b, i: (b, 0, i)),
        out_shape=jax.ShapeDtypeStruct((bsz, ne, seq), F32),
        compiler_params=_cparams("parallel", "arbitrary"),
        name="router",
    )(h, g.reshape(1, d), w_router.T)


def _plan_kernel(aff_ref, tri_ref, pfx_ref, idx_ref, gate_ref, key_ref, *, cap, slot_block):
    ne, seq = aff_ref.shape
    nch = seq // CUMSUM_CHUNK
    a = aff_ref[...]

    def count(mask):
        return jnp.sum(jnp.where(mask, 1.0, 0.0), axis=1, keepdims=True)

    def search(j, thr_bits):
        cand = thr_bits | lax.shift_left(jnp.int32(1), 30 - j)
        return jnp.where(count(a >= pltpu.bitcast(cand, F32)) >= cap, cand, thr_bits)

    thr = pltpu.bitcast(lax.fori_loop(0, 31, search, jnp.zeros((ne, 1), jnp.int32)), F32)

    def cumsum(m):
        st = jnp.concatenate([m[:, c * CUMSUM_CHUNK:(c + 1) * CUMSUM_CHUNK] for c in range(nch)], axis=0)
        within = _dot(st.astype(BF16), tri_ref[...])
        tot = jnp.broadcast_to(within[:, CUMSUM_CHUNK - 1:CUMSUM_CHUNK], (nch * ne, LANES))
        offs = _dot(pfx_ref[...], tot.astype(BF16))
        full = within + offs[:, 0:1]
        return jnp.concatenate([full[c * ne:(c + 1) * ne, :] for c in range(nch)], axis=1)

    gt = a > thr
    eq = a == thr
    need = cap - count(gt)
    eq_rank = cumsum(jnp.where(eq, 1.0, 0.0))
    sel = gt | (eq & (eq_rank <= need))
    rank = cumsum(jnp.where(sel, 1.0, 0.0))
    key_ref[...] = jnp.where(sel, rank, 0.0)

    nblk = cap // slot_block
    slot1 = (lax.broadcasted_iota(jnp.int32, (slot_block, LANES), 0) + 1).astype(F32)
    lane = lax.broadcasted_iota(jnp.int32, (1, LANES), 1).astype(F32)
    ones = jnp.ones((8, LANES), F32)

    def per_expert(e, carry):
        key_row = key_ref[pl.ds(e, 1), :]
        aff_row = aff_ref[pl.ds(e, 1), :]
        for pb in range(nblk):
            want = slot1 + float(pb * slot_block)
            acc_t = jnp.zeros((slot_block, LANES), F32)
            acc_g = jnp.zeros((slot_block, LANES), F32)
            for c in range(seq // LANES):
                hit = key_row[:, c * LANES:(c + 1) * LANES] == want
                acc_t = acc_t + jnp.where(hit, lane + float(c * LANES), 0.0)
                acc_g = acc_g + jnp.where(hit, aff_row[:, c * LANES:(c + 1) * LANES], 0.0)
            tok = _dot_nt(ones, acc_t, precision=lax.Precision.HIGHEST)
            idx_ref[e, :, pb * slot_block:(pb + 1) * slot_block] = tok[0:1, :].astype(jnp.int32)
            gate = jnp.sum(acc_g, axis=1, keepdims=True)
            gate_ref[e, pb * slot_block:(pb + 1) * slot_block, :] = jnp.broadcast_to(gate, (slot_block, LANES))
        return carry

    lax.fori_loop(0, ne, per_expert, 0)


def _plan(aff, *, cap):
    bsz, ne, seq = aff.shape
    nch = seq // CUMSUM_CHUNK
    r = np.arange(CUMSUM_CHUNK)
    tri = jnp.asarray(r[:, None] <= r[None, :], BF16)
    q = np.arange(nch * ne)
    pfx = jnp.asarray((q[:, None] % ne == q[None, :] % ne) & (q[None, :] // ne < q[:, None] // ne), BF16)
    slot_block = min(cap, LANES)
    return pl.pallas_call(
        functools.partial(_plan_kernel, cap=cap, slot_block=slot_block),
        grid=(bsz,),
        in_specs=[
            pl.BlockSpec((None, ne, seq), lambda b: (b, 0, 0)),
            pl.BlockSpec((CUMSUM_CHUNK, CUMSUM_CHUNK), lambda b: (0, 0)),
            pl.BlockSpec((nch * ne, nch * ne), lambda b: (0, 0)),
        ],
        out_specs=[
            pl.BlockSpec((None, ne, 1, cap), lambda b: (b, 0, 0, 0)),
            pl.BlockSpec((None, ne, cap, LANES), lambda b: (b, 0, 0, 0)),
        ],
        out_shape=[
            jax.ShapeDtypeStruct((bsz, ne, 1, cap), jnp.int32),
            jax.ShapeDtypeStruct((bsz, ne, cap, LANES), F32),
        ],
        scratch_shapes=[pltpu.VMEM((ne, seq), F32)],
        compiler_params=_cparams("parallel"),
        name="route_plan",
    )(aff, tri, pfx)


ROW_UNROLL = 8


def _gather_kernel(idx_ref, h_ref, g_ref, xg_ref, rows_ref):
    b, e = pl.program_id(0), pl.program_id(1)
    cap = rows_ref.shape[0]
    base = (b * pl.num_programs(1) + e) * cap

    def move(i, carry):
        for r in range(ROW_UNROLL):
            p = i * ROW_UNROLL + r
            rows_ref[pl.ds(p, 1), :] = h_ref[pl.ds(idx_ref[base + p], 1), :]
        return carry

    lax.fori_loop(0, cap // ROW_UNROLL, move, 0)
    xg_ref[...] = _rms(rows_ref[...], g_ref[...]).astype(BF16)


def _gather(idx_flat, h, g, *, ne, cap):
    bsz, seq, d = h.shape
    return pl.pallas_call(
        _gather_kernel,
        grid_spec=pltpu.PrefetchScalarGridSpec(
            num_scalar_prefetch=1,
            grid=(bsz, ne),
            in_specs=[
                pl.BlockSpec((None, seq, d), lambda b, e, idx: (b, 0, 0)),
                pl.BlockSpec((1, d), lambda b, e, idx: (0, 0)),
            ],
            out_specs=pl.BlockSpec((None, None, cap, d), lambda b, e, idx: (e, b, 0, 0)),
            scratch_shapes=[pltpu.VMEM((cap, d), F32)],
        ),
        out_shape=jax.ShapeDtypeStruct((ne, bsz, cap, d), BF16),
        compiler_params=_cparams("parallel", "arbitrary"),
        name="moe_gather",
    )(idx_flat, h, g.reshape(1, d))


def _ffn_kernel(x_ref, gate_ref, wg_ref, wu_ref, wd_ref, y_ref, wgb_ref, wub_ref, wdb_ref, *, sub):
    f = pl.program_id(2)
    nf = pl.num_programs(2)
    wgb_ref[...] = wg_ref[...].astype(BF16)
    wub_ref[...] = wu_ref[...].astype(BF16)
    wdb_ref[...] = wd_ref[...].astype(BF16)

    @pl.when(f == 0)
    def _():
        y_ref[...] = jnp.zeros(y_ref.shape, F32)

    nb, cap = gate_ref.shape[0], gate_ref.shape[1]
    for bi in range(nb):
        for s in range(cap // sub):
            r0 = bi * cap + s * sub
            x = x_ref[r0:r0 + sub, :]
            h1 = _dot(x, wgb_ref[...])
            h2 = _dot(x, wub_ref[...])
            act = (h1 * jax.nn.sigmoid(h1) * h2).astype(BF16)
            y = y_ref[r0:r0 + sub, :] + _dot(act, wdb_ref[...])
            gate = gate_ref[bi, s * sub:(s + 1) * sub, 0:1]
            y_ref[r0:r0 + sub, :] = jnp.where(f == nf - 1, y * gate, y)


def _ffn(xg, gateb, w_gate, w_up, w_down, *, rows, fchunk, sub):
    ne, bsz, cap, d = xg.shape
    m = bsz * cap
    dff = w_gate.shape[2]
    nb = rows // cap
    x2 = xg.reshape(ne, m, d)
    return pl.pallas_call(
        functools.partial(_ffn_kernel, sub=sub),
        grid=(ne, m // rows, dff // fchunk),
        in_specs=[
            pl.BlockSpec((None, rows, d), lambda e, r, f: (e, r, 0)),
            pl.BlockSpec((nb, None, cap, LANES), lambda e, r, f: (r, e, 0, 0)),
            pl.BlockSpec((None, d, fchunk), lambda e, r, f: (e, 0, f)),
            pl.BlockSpec((None, d, fchunk), lambda e, r, f: (e, 0, f)),
            pl.BlockSpec((None, fchunk, d), lambda e, r, f: (e, f, 0)),
        ],
        out_specs=pl.BlockSpec((None, rows, d), lambda e, r, f: (e, r, 0)),
        out_shape=jax.ShapeDtypeStruct((ne, m, d), F32),
        scratch_shapes=[
            pltpu.VMEM((d, fchunk), BF16),
            pltpu.VMEM((d, fchunk), BF16),
            pltpu.VMEM((fchunk, d), BF16),
        ],
        compiler_params=_cparams("parallel", "parallel", "arbitrary"),
        name="moe_experts",
    )(x2, gateb, w_gate, w_up, w_down)


def _scatter_kernel(idx_ref, h_ref, y_ref, g_ref, o_ref, *, final_norm, norm_rows):
    b, e = pl.program_id(0), pl.program_id(1)
    ne = pl.num_programs(1)
    cap = y_ref.shape[0]
    base = (b * ne + e) * cap
    part = h_ref.shape[0]

    @pl.when(e == 0)
    def _():
        o_ref[...] = jnp.zeros(o_ref.shape, F32)

    r_lo = pl.multiple_of(e * part, part)
    o_ref[pl.ds(r_lo, part), :] = o_ref[pl.ds(r_lo, part), :] + h_ref[...]

    def add(i, carry):
        p0 = i * ROW_UNROLL
        toks = [idx_ref[base + p0 + r] for r in range(ROW_UNROLL)]
        vals = [o_ref[pl.ds(toks[r], 1), :] + y_ref[pl.ds(p0 + r, 1), :] for r in range(ROW_UNROLL)]
        for r in range(ROW_UNROLL):
            o_ref[pl.ds(toks[r], 1), :] = vals[r]
        return carry

    lax.fori_loop(0, cap // ROW_UNROLL, add, 0)

    if final_norm:
        @pl.when(e == ne - 1)
        def _():
            g = g_ref[...]

            def norm(i, carry):
                r0 = pl.multiple_of(i * norm_rows, norm_rows)
                o_ref[pl.ds(r0, norm_rows), :] = _rms(o_ref[pl.ds(r0, norm_rows), :], g)
                return carry

            lax.fori_loop(0, o_ref.shape[0] // norm_rows, norm, 0)


def _scatter(idx_flat, h, y, g_final, *, ne, cap, final_norm):
    bsz, seq, d = h.shape
    y4 = y.reshape(ne, bsz, cap, d)
    part = seq // ne
    return pl.pallas_call(
        functools.partial(_scatter_kernel, final_norm=final_norm, norm_rows=min(seq, 256)),
        grid_spec=pltpu.PrefetchScalarGridSpec(
            num_scalar_prefetch=1,
            grid=(bsz, ne),
            in_specs=[
                pl.BlockSpec((None, part, d), lambda b, e, idx: (b, e, 0)),
                pl.BlockSpec((None, None, cap, d), lambda b, e, idx: (e, b, 0, 0)),
                pl.BlockSpec((1, d), lambda b, e, idx: (0, 0)),
            ],
            out_specs=pl.BlockSpec((None, seq, d), lambda b, e, idx: (b, 0, 0)),
        ),
        out_shape=jax.ShapeDtypeStruct((bsz, seq, d), F32),
        compiler_params=_cparams("parallel", "arbitrary"),
        name="moe_scatter",
    )(idx_flat, h, y4, g_final.reshape(1, d))


def _moe_layer(h, g, w_router, w_gate, w_up, w_down, g_final, *, final_norm, tiles):
    bsz, seq, d = h.shape
    ne = w_router.shape[1]
    cap = CAPACITY_FACTOR * seq // ne
    aff = _router(h, g, w_router, tile=tiles["router"])
    idx, gateb = _plan(aff, cap=cap)
    idx_flat = idx.reshape(-1)
    xg = _gather(idx_flat, h, g, ne=ne, cap=cap)
    y = _ffn(xg, gateb, w_gate, w_up, w_down, rows=tiles["ffn_rows"], fchunk=tiles["ffn_f"], sub=tiles["ffn_sub"])
    return _scatter(idx_flat, h, y, g_final, ne=ne, cap=cap, final_norm=final_norm)


def _tiles(bsz, seq, d, dff, cap):
    m = bsz * cap
    rows = min(m, 2048)
    rows = max(cap, rows // cap * cap)
    return {
        "conv": min(seq, 512),
        "qkv": min(seq, 512),
        "tq": min(seq, 512),
        "tk": min(seq, 512),
        "oproj": min(seq, 1024),
        "router": min(seq, 1024),
        "ffn_rows": rows,
        "ffn_f": min(dff, 512),
        "ffn_sub": min(cap, 512),
    }


def kernel(x, norm_mix, norm_ffn, conv_in, conv_w, conv_out, attn_qkv, attn_q_norm, attn_k_norm, attn_out,
           router, w_gate, w_up, w_down, final_norm):
    bsz, seq, d = x.shape
    depth = norm_mix.shape[0]
    ne = router.shape[2]
    tiles = _tiles(bsz, seq, d, w_gate.shape[3], CAPACITY_FACTOR * seq // ne)
    h = x
    for i in range(depth):
        j = i // 2
        if i % 2 == 0:
            h = _conv_layer(h, norm_mix[i], conv_in[j], conv_w[j], conv_out[j], tile=tiles["conv"])
        else:
            hd = attn_q_norm.shape[1]
            q, k2, v2 = _qkv_layer(h, norm_mix[i], attn_qkv[j], attn_q_norm[j], attn_k_norm[j], tile=tiles["qkv"])
            o = _attention(q, k2, v2, hd=hd, tq=tiles["tq"], tk=tiles["tk"])
            h = _oproj_layer(o, h, attn_out[j], tile=tiles["oproj"])
        h = _moe_layer(h, norm_ffn[i], router[i], w_gate[i], w_up[i], w_down[i], final_norm,
                       final_norm=(i == depth - 1), tiles=tiles)
    if depth == 0:
        raise NotImplementedError("depth 0")
    return h
```

```python
import functools

import jax
import jax.numpy as jnp
import numpy as np
from jax import lax
from jax.experimental import pallas as pl
from jax.experimental.pallas import tpu as pltpu

EPS = 1e-6
N_HEADS = 16
N_KV_HEADS = 4
GRID_W = 64
ROPE_THETA = 10000.0
CAPACITY_FACTOR = 2
LOG2E = 1.4426950408889634
ATTN_LOOKAHEAD = 2

LANES = 128
CUMSUM_CHUNK = 256
VMEM_LIMIT = 56 * 1024 * 1024

F32 = jnp.float32
BF16 = jnp.bfloat16


def _cparams(*sem):
    return pltpu.CompilerParams(dimension_semantics=sem, vmem_limit_bytes=VMEM_LIMIT)


def _rms(x, g):
    return x * lax.rsqrt(jnp.mean(x * x, axis=-1, keepdims=True) + EPS) * g


def _dot(a, b):
    return jnp.dot(a, b, preferred_element_type=F32)


def _dot_nt(a, b, precision=None):
    return lax.dot_general(a, b, (((1,), (1,)), ((), ())), precision=precision,
                           preferred_element_type=F32)


HALO = 8


def _conv_kernel(xp_ref, x_ref, xq_ref, g_ref, win_ref, wc_ref, wout_ref, o_ref, u_ref):
    i = pl.program_id(1)
    n = pl.num_programs(1)
    t = x_ref.shape[0]
    d = x_ref.shape[1]
    x = x_ref[...]
    g = g_ref[...]
    xs = jnp.concatenate([_rms(xp_ref[...], g), _rms(x, g), _rms(xq_ref[...], g)], axis=0).astype(BF16)
    c = _dot(xs, win_ref[:, d:2 * d])
    h = _dot(xs, win_ref[:, 2 * d:3 * d])
    u = c * h
    row = lax.broadcasted_iota(jnp.int32, (t + 2 * HALO, 1), 0)
    first_live = jnp.where(i == 0, HALO, 0)
    end_live = jnp.where(i == n - 1, HALO + t, t + 2 * HALO)
    u_ref[...] = jnp.where((row >= first_live) & (row < end_live), u, 0.0)
    wc = wc_ref[...]
    y = (wc[0:1, :] * u_ref[HALO - 1:HALO - 1 + t, :] + wc[1:2, :] * u_ref[HALO:HALO + t, :]
         + wc[2:3, :] * u_ref[HALO + 1:HALO + 1 + t, :])
    b = _dot(xs, win_ref[:, 0:d])[HALO:HALO + t, :]
    o_ref[...] = x + _dot((b * y).astype(BF16), wout_ref[...])


def _conv_layer(h, g, w_in, w_conv, w_out, *, tile):
    bsz, seq, d = h.shape
    tb = tile // HALO
    nb = seq // HALO
    return pl.pallas_call(
        _conv_kernel,
        grid=(bsz, seq // tile),
        in_specs=[
            pl.BlockSpec((None, HALO, d), lambda b, i: (b, jnp.maximum(i * tb - 1, 0), 0)),
            pl.BlockSpec((None, tile, d), lambda b, i: (b, i, 0)),
            pl.BlockSpec((None, HALO, d), lambda b, i: (b, jnp.minimum((i + 1) * tb, nb - 1), 0)),
            pl.BlockSpec((1, d), lambda b, i: (0, 0)),
            pl.BlockSpec((d, 3 * d), lambda b, i: (0, 0)),
            pl.BlockSpec((3, d), lambda b, i: (0, 0)),
            pl.BlockSpec((d, d), lambda b, i: (0, 0)),
        ],
        out_specs=pl.BlockSpec((None, tile, d), lambda b, i: (b, i, 0)),
        out_shape=jax.ShapeDtypeStruct((bsz, seq, d), F32),
        scratch_shapes=[pltpu.VMEM((tile + 2 * HALO, d), F32)],
        compiler_params=_cparams("parallel", "arbitrary"),
        name="conv_mixer",
    )(h, h, h, g.reshape(1, d), w_in.astype(BF16), w_conv, w_out.astype(BF16))


def _rope_tables(seq, head_dim, g, scale):
    axis = head_dim // 2
    quarter = axis // 2
    rows = seq // GRID_W
    row = jnp.repeat(jnp.arange(rows, dtype=F32), GRID_W)
    col = jnp.tile(jnp.arange(GRID_W, dtype=F32), rows)
    inv_freq = ROPE_THETA ** (-jnp.arange(0, axis, 2, dtype=F32) / axis)
    j = np.arange(head_dim)
    pos = jnp.where((j < axis)[None, :], row[:, None], col[:, None])
    ang = pos * inv_freq[j % quarter][None, :]
    first = (j % axis) < quarter
    partner = np.where(first, j + quarter, j - quarter)
    cos_t = jnp.cos(ang) * g[None, :] * scale
    sin_t = jnp.sin(ang) * jnp.where(first, -1.0, 1.0)[None, :] * g[partner][None, :] * scale
    return jnp.tile(cos_t, (1, 2)), jnp.tile(sin_t, (1, 2)), partner


def _qkv_kernel(x_ref, g_ref, w_ref, cq_ref, sq_ref, ck_ref, sk_ref, q_ref, k_ref, v_ref, *, hd):
    t, d = x_ref.shape
    nq = q_ref.shape[1]
    nk = k_ref.shape[1] // 2
    xn = _rms(x_ref[...], g_ref[...]).astype(BF16)
    lane = lax.broadcasted_iota(jnp.int32, (1, LANES), 1)
    lo = lane < hd

    def head_scale(raw):
        sq = raw * raw
        s0 = jnp.sum(jnp.where(lo, sq, 0.0), axis=-1, keepdims=True)
        s1 = jnp.sum(jnp.where(lo, 0.0, sq), axis=-1, keepdims=True)
        return jnp.where(lo, lax.rsqrt(s0 / hd + EPS), lax.rsqrt(s1 / hd + EPS))

    def dup(pair):
        sw = pltpu.roll(pair, hd, axis=1)
        return jnp.where(lo, pair, sw), jnp.where(lo, sw, pair)

    cq, sq_t = cq_ref[...], sq_ref[...]
    for j in range(nq // LANES):
        raw = _dot(xn, w_ref[:, j * LANES:(j + 1) * LANES])
        swp = _dot(xn, w_ref[:, nq + j * LANES:nq + (j + 1) * LANES])
        q_ref[:, j * LANES:(j + 1) * LANES] = ((raw * cq + swp * sq_t) * head_scale(raw)).astype(BF16)
    ck, sk_t = ck_ref[...], sk_ref[...]
    for j in range(nk // LANES):
        raw = _dot(xn, w_ref[:, 2 * nq + j * LANES:2 * nq + (j + 1) * LANES])
        swp = _dot(xn, w_ref[:, 2 * nq + nk + j * LANES:2 * nq + nk + (j + 1) * LANES])
        ka, kb = dup(((raw * ck + swp * sk_t) * head_scale(raw)).astype(F32))
        k_ref[:, 2 * j * LANES:(2 * j + 1) * LANES] = ka.astype(BF16)
        k_ref[:, (2 * j + 1) * LANES:(2 * j + 2) * LANES] = kb.astype(BF16)
        va, vb = dup(_dot(xn, w_ref[:, 2 * nq + 2 * nk + j * LANES:2 * nq + 2 * nk + (j + 1) * LANES]))
        v_ref[:, 2 * j * LANES:(2 * j + 1) * LANES] = va.astype(BF16)
        v_ref[:, (2 * j + 1) * LANES:(2 * j + 2) * LANES] = vb.astype(BF16)


def _qkv_layer(h, g, w_qkv, g_q, g_k, *, tile):
    bsz, seq, d = h.shape
    hd = g_q.shape[0]
    assert 2 * hd == LANES
    nq, nk = N_HEADS * hd, N_KV_HEADS * hd
    cq, sq, partner = _rope_tables(seq, hd, g_q, hd ** -0.5 * LOG2E)
    ck, sk, _ = _rope_tables(seq, hd, g_k, 1.0)
    wq, wk, wv = w_qkv[:, :nq], w_qkv[:, nq:nq + nk], w_qkv[:, nq + nk:]
    perm_q = (np.arange(nq) // hd) * hd + partner[np.arange(nq) % hd]
    perm_k = (np.arange(nk) // hd) * hd + partner[np.arange(nk) % hd]
    w_all = jnp.concatenate([wq, wq[:, perm_q], wk, wk[:, perm_k], wv], axis=1).astype(BF16)
    ncol = w_all.shape[1]
    tab = pl.BlockSpec((tile, LANES), lambda b, i: (i, 0))
    return pl.pallas_call(
        functools.partial(_qkv_kernel, hd=hd),
        grid=(bsz, seq // tile),
        in_specs=[
            pl.BlockSpec((None, tile, d), lambda b, i: (b, i, 0)),
            pl.BlockSpec((1, d), lambda b, i: (0, 0)),
            pl.BlockSpec((d, ncol), lambda b, i: (0, 0)),
            tab, tab, tab, tab,
        ],
        out_specs=[
            pl.BlockSpec((None, tile, nq), lambda b, i: (b, i, 0)),
            pl.BlockSpec((None, tile, 2 * nk), lambda b, i: (b, i, 0)),
            pl.BlockSpec((None, tile, 2 * nk), lambda b, i: (b, i, 0)),
        ],
        out_shape=[
            jax.ShapeDtypeStruct((bsz, seq, nq), BF16),
            jax.ShapeDtypeStruct((bsz, seq, 2 * nk), BF16),
            jax.ShapeDtypeStruct((bsz, seq, 2 * nk), BF16),
        ],
        compiler_params=_cparams("parallel", "arbitrary"),
        name="qkv_rope",
    )(h, g.reshape(1, d), w_all, cq, sq, ck, sk)


def _attn_kernel(q_ref, k_ref, v_ref, o_ref, *, hd, rows, tk):
    nh = q_ref.shape[1] // hd
    tq = q_ref.shape[0]
    seq = k_ref.shape[0]
    lane = lax.broadcasted_iota(jnp.int32, (1, LANES), 1)
    lo = lane < hd
    zero = jnp.zeros((), BF16)
    one = jnp.ones((), BF16)

    for r in range(tq // rows):
        qs = []
        for h in range(nh):
            mine = lo if h % 2 == 0 else jnp.logical_not(lo)
            qs.append(jnp.where(mine, q_ref[r * rows:(r + 1) * rows, (h // 2) * LANES:(h // 2 + 1) * LANES], zero))

        stages = [(kc, h) for kc in range(seq // tk) for h in range(nh)]
        m = [jnp.full((rows, LANES), -jnp.inf, F32)] * nh
        acc = [jnp.zeros((rows, LANES), F32)] * nh
        scores, v_aug = {}, {}
        for i in range(len(stages) + ATTN_LOOKAHEAD):
            if i < len(stages):
                kc, h = stages[i]
                scores[i] = _dot_nt(qs[h], k_ref[kc * tk:(kc + 1) * tk, :])
            j = i - ATTN_LOOKAHEAD
            if j >= 0:
                kc, h = stages[j]
                if kc not in v_aug:
                    v2 = v_ref[kc * tk:(kc + 1) * tk, :]
                    v_aug = {kc: (jnp.where(lo, v2, one), jnp.where(lo, one, v2))}
                s = scores.pop(j)
                m_new = jnp.maximum(m[h], jnp.max(s, axis=-1, keepdims=True))
                p = jnp.exp2(s - jnp.concatenate([m_new] * (tk // LANES), axis=1))
                acc[h] = acc[h] * jnp.exp2(m[h] - m_new) + _dot(p.astype(BF16), v_aug[kc][h % 2])
                m[h] = m_new
        for pair in range(nh // 2):
            a0, a1 = acc[2 * pair], acc[2 * pair + 1]
            num = jnp.where(lo, a0, a1)
            den = jnp.where(lo, pltpu.roll(a0, hd, axis=1), pltpu.roll(a1, hd, axis=1))
            o_ref[r * rows:(r + 1) * rows, pair * LANES:(pair + 1) * LANES] = (num / den).astype(o_ref.dtype)


def _attention(q, k2, v2, *, hd, tq, rows, tk):
    bsz, seq, nq = q.shape
    group = N_HEADS // N_KV_HEADS
    gw = group * hd
    return pl.pallas_call(
        functools.partial(_attn_kernel, hd=hd, rows=rows, tk=tk),
        grid=(bsz, N_KV_HEADS, seq // tq),
        in_specs=[
            pl.BlockSpec((None, tq, gw), lambda b, g, qi: (b, qi, g)),
            pl.BlockSpec((None, seq, LANES), lambda b, g, qi: (b, 0, g)),
            pl.BlockSpec((None, seq, LANES), lambda b, g, qi: (b, 0, g)),
        ],
        out_specs=pl.BlockSpec((None, tq, gw), lambda b, g, qi: (b, qi, g)),
        out_shape=jax.ShapeDtypeStruct((bsz, seq, nq), BF16),
        compiler_params=_cparams("parallel", "parallel", "arbitrary"),
        name="flash_attn",
    )(q, k2, v2)


def _oproj_kernel(o_ref, h_ref, w_ref, out_ref):
    out_ref[...] = h_ref[...] + _dot(o_ref[...], w_ref[...])


def _oproj_layer(o, h, w_o, *, tile):
    bsz, seq, d = h.shape
    nq = o.shape[2]
    return pl.pallas_call(
        _oproj_kernel,
        grid=(bsz, seq // tile),
        in_specs=[
            pl.BlockSpec((None, tile, nq), lambda b, i: (b, i, 0)),
            pl.BlockSpec((None, tile, d), lambda b, i: (b, i, 0)),
            pl.BlockSpec((nq, d), lambda b, i: (0, 0)),
        ],
        out_specs=pl.BlockSpec((None, tile, d), lambda b, i: (b, i, 0)),
        out_shape=jax.ShapeDtypeStruct((bsz, seq, d), F32),
        compiler_params=_cparams("parallel", "arbitrary"),
        name="attn_out_proj",
    )(o, h, w_o.astype(BF16))


def _router_kernel(h_ref, g_ref, wt_ref, aff_ref):
    xn = _rms(h_ref[...], g_ref[...])
    lg = _dot_nt(wt_ref[...], xn, precision=lax.Precision.HIGHEST)
    ex = jnp.exp(lg - jnp.max(lg, axis=0, keepdims=True))
    aff_ref[...] = ex / jnp.sum(ex, axis=0, keepdims=True)


def _router(h, g, w_router, *, tile):
    bsz, seq, d = h.shape
    ne = w_router.shape[1]
    return pl.pallas_call(
        _router_kernel,
        grid=(bsz, seq // tile),
        in_specs=[
            pl.BlockSpec((None, tile, d), lambda b, i: (b, i, 0)),
            pl.BlockSpec((1, d), lambda b, i: (0, 0)),
            pl.BlockSpec((ne, d), lambda b, i: (0, 0)),
        ],
        out_specs=pl.BlockSpec((None, ne, tile), lambda b, i: (b, 0, i)),
        out_shape=jax.ShapeDtypeStruct((bsz, ne, seq), F32),
        compiler_params=_cparams("parallel", "arbitrary"),
        name="router",
    )(h, g.reshape(1, d), w_router.T)


def _plan_kernel(aff_ref, tri_ref, pfx_ref, idx_ref, gate_ref, key_ref, affc_ref, filled_ref, *, cap, slot_block):
    ne, seq = aff_ref.shape
    nch = seq // CUMSUM_CHUNK
    a = aff_ref[...]

    def count(mask):
        return jnp.sum(jnp.where(mask, 1.0, 0.0), axis=1, keepdims=True)

    def search(j, thr_bits):
        cand = thr_bits | lax.shift_left(jnp.int32(1), 30 - j)
        return jnp.where(count(a >= pltpu.bitcast(cand, F32)) >= cap, cand, thr_bits)

    thr = pltpu.bitcast(lax.fori_loop(0, 31, search, jnp.zeros((ne, 1), jnp.int32)), F32)

    def cumsum(m):
        st = jnp.concatenate([m[:, c * CUMSUM_CHUNK:(c + 1) * CUMSUM_CHUNK] for c in range(nch)], axis=0)
        within = _dot(st.astype(BF16), tri_ref[...])
        tot = jnp.broadcast_to(within[:, CUMSUM_CHUNK - 1:CUMSUM_CHUNK], (nch * ne, LANES))
        offs = _dot(pfx_ref[...], tot.astype(BF16))
        full = within + offs[:, 0:1]
        return jnp.concatenate([full[c * ne:(c + 1) * ne, :] for c in range(nch)], axis=1)

    gt = a > thr
    eq = a == thr
    need = cap - count(gt)
    eq_rank = cumsum(jnp.where(eq, 1.0, 0.0))
    sel = gt | (eq & (eq_rank <= need))
    rank = cumsum(jnp.where(sel, 1.0, 0.0))
    key = jnp.where(sel, rank, 0.0)

    nck = seq // LANES
    lane_i = lax.broadcasted_iota(jnp.int32, (1, LANES), 1)
    filled = jnp.full((ne, LANES), 1e9, F32)
    for c in range(nck):
        key_ref[c] = key[:, c * LANES:(c + 1) * LANES]
        affc_ref[c] = a[:, c * LANES:(c + 1) * LANES]
        filled = jnp.where(lane_i == c, jnp.max(rank[:, c * LANES:(c + 1) * LANES], axis=1, keepdims=True), filled)
    filled_ref[...] = filled

    nblk = cap // slot_block
    slot1 = (lax.broadcasted_iota(jnp.int32, (slot_block, LANES), 0) + 1).astype(F32)
    lane = lane_i.astype(F32)
    ones = jnp.ones((8, LANES), F32)

    def scalar_count(mask):
        return jnp.sum(jnp.where(mask, 1.0, 0.0), axis=1, keepdims=True).astype(jnp.int32)[0, 0]

    def per_expert(e, carry):
        filled_row = filled_ref[pl.ds(e, 1), :]
        bounds = [(scalar_count(filled_row <= float(pb * slot_block)),
                   jnp.minimum(scalar_count(filled_row < float((pb + 1) * slot_block)) + 1, nck))
                  for pb in range(nblk)]
        accs_t, accs_g = [], []
        for pb in range(nblk):
            want = slot1 + float(pb * slot_block)

            def chunk(c, accs, want=want):
                hit = key_ref[c, pl.ds(e, 1), :] == want
                tok = lane + (c * LANES).astype(F32)
                return (accs[0] + jnp.where(hit, tok, 0.0),
                        accs[1] + jnp.where(hit, affc_ref[c, pl.ds(e, 1), :], 0.0))

            zeros = jnp.zeros((slot_block, LANES), F32)
            acc_t, acc_g = lax.fori_loop(bounds[pb][0], bounds[pb][1], chunk, (zeros, zeros))
            accs_t.append(acc_t)
            accs_g.append(acc_g)
        tok = _dot_nt(ones, jnp.concatenate(accs_t, axis=0), precision=lax.Precision.HIGHEST)
        idx_ref[e] = tok[0:1, :].astype(jnp.int32)
        gate = jnp.sum(jnp.concatenate(accs_g, axis=0), axis=1, keepdims=True)
        gate_ref[e] = jnp.broadcast_to(gate, (cap, LANES))
        return carry

    lax.fori_loop(0, ne, per_expert, 0)


def _plan(aff, *, cap):
    bsz, ne, seq = aff.shape
    nch = seq // CUMSUM_CHUNK
    r = np.arange(CUMSUM_CHUNK)
    tri = jnp.asarray(r[:, None] <= r[None, :], BF16)
    q = np.arange(nch * ne)
    pfx = jnp.asarray((q[:, None] % ne == q[None, :] % ne) & (q[None, :] // ne < q[:, None] // ne), BF16)
    slot_block = min(cap, LANES)
    return pl.pallas_call(
        functools.partial(_plan_kernel, cap=cap, slot_block=slot_block),
        grid=(bsz,),
        in_specs=[
            pl.BlockSpec((None, ne, seq), lambda b: (b, 0, 0)),
            pl.BlockSpec((CUMSUM_CHUNK, CUMSUM_CHUNK), lambda b: (0, 0)),
            pl.BlockSpec((nch * ne, nch * ne), lambda b: (0, 0)),
        ],
        out_specs=[
            pl.BlockSpec((None, ne, 1, cap), lambda b: (b, 0, 0, 0)),
            pl.BlockSpec((None, ne, cap, LANES), lambda b: (b, 0, 0, 0)),
        ],
        out_shape=[
            jax.ShapeDtypeStruct((bsz, ne, 1, cap), jnp.int32),
            jax.ShapeDtypeStruct((bsz, ne, cap, LANES), F32),
        ],
        scratch_shapes=[
            pltpu.VMEM((seq // LANES, ne, LANES), F32),
            pltpu.VMEM((seq // LANES, ne, LANES), F32),
            pltpu.VMEM((ne, LANES), F32),
        ],
        compiler_params=_cparams("parallel"),
        name="route_plan",
    )(aff, tri, pfx)


ROW_UNROLL = 8


def _gather_kernel(idx_ref, h_ref, g_ref, xg_ref, rows_ref):
    b, e = pl.program_id(0), pl.program_id(1)
    cap, d = xg_ref.shape
    base = (b * pl.num_programs(1) + e) * cap

    def move(i, carry):
        for r in range(ROW_UNROLL):
            rows_ref[i, r:r + 1, :] = h_ref[pl.ds(idx_ref[base + i * ROW_UNROLL + r], 1), :]
        return carry

    lax.fori_loop(0, cap // ROW_UNROLL, move, 0)
    xg_ref[...] = _rms(rows_ref[...].reshape(cap, d), g_ref[...]).astype(BF16)


def _gather(idx_flat, h, g, *, ne, cap):
    bsz, seq, d = h.shape
    return pl.pallas_call(
        _gather_kernel,
        grid_spec=pltpu.PrefetchScalarGridSpec(
            num_scalar_prefetch=1,
            grid=(bsz, ne),
            in_specs=[
                pl.BlockSpec((None, seq, d), lambda b, e, idx: (b, 0, 0)),
                pl.BlockSpec((1, d), lambda b, e, idx: (0, 0)),
            ],
            out_specs=pl.BlockSpec((None, None, cap, d), lambda b, e, idx: (e, b, 0, 0)),
            scratch_shapes=[pltpu.VMEM((cap // ROW_UNROLL, ROW_UNROLL, d), F32)],
        ),
        out_shape=jax.ShapeDtypeStruct((ne, bsz, cap, d), BF16),
        compiler_params=_cparams("parallel", "arbitrary"),
        name="moe_gather",
    )(idx_flat, h, g.reshape(1, d))


def _ffn_kernel(x_ref, gate_ref, wg_ref, wu_ref, wd_ref, y_ref, wgb_ref, wub_ref, wdb_ref, *, sub):
    f = pl.program_id(2)
    nf = pl.num_programs(2)
    wgb_ref[...] = wg_ref[...].astype(BF16)
    wub_ref[...] = wu_ref[...].astype(BF16)
    wdb_ref[...] = wd_ref[...].astype(BF16)

    @pl.when(f == 0)
    def _():
        y_ref[...] = jnp.zeros(y_ref.shape, F32)

    nb, cap = gate_ref.shape[0], gate_ref.shape[1]
    for bi in range(nb):
        for s in range(cap // sub):
            r0 = bi * cap + s * sub
            x = x_ref[r0:r0 + sub, :]
            h1 = _dot(x, wgb_ref[...])
            h2 = _dot(x, wub_ref[...])
            act = (h1 * jax.nn.sigmoid(h1) * h2).astype(BF16)
            y = y_ref[r0:r0 + sub, :] + _dot(act, wdb_ref[...])
            gate = gate_ref[bi, s * sub:(s + 1) * sub, 0:1]
            y_ref[r0:r0 + sub, :] = jnp.where(f == nf - 1, y * gate, y)


def _ffn(xg, gateb, w_gate, w_up, w_down, *, layer, rows, fchunk, sub):
    ne, bsz, cap, d = xg.shape
    m = bsz * cap
    dff = w_gate.shape[3]
    nb = rows // cap
    x2 = xg.reshape(ne, m, d)
    return pl.pallas_call(
        functools.partial(_ffn_kernel, sub=sub),
        grid=(ne, m // rows, dff // fchunk),
        in_specs=[
            pl.BlockSpec((None, rows, d), lambda e, r, f: (e, r, 0)),
            pl.BlockSpec((nb, None, cap, LANES), lambda e, r, f: (r, e, 0, 0)),
            pl.BlockSpec((None, None, d, fchunk), lambda e, r, f: (layer, e, 0, f)),
            pl.BlockSpec((None, None, d, fchunk), lambda e, r, f: (layer, e, 0, f)),
            pl.BlockSpec((None, None, fchunk, d), lambda e, r, f: (layer, e, f, 0)),
        ],
        out_specs=pl.BlockSpec((None, rows, d), lambda e, r, f: (e, r, 0)),
        out_shape=jax.ShapeDtypeStruct((ne, m, d), F32),
        scratch_shapes=[
            pltpu.VMEM((d, fchunk), BF16),
            pltpu.VMEM((d, fchunk), BF16),
            pltpu.VMEM((fchunk, d), BF16),
        ],
        compiler_params=_cparams("parallel", "parallel", "arbitrary"),
        name="moe_experts",
    )(x2, gateb, w_gate, w_up, w_down)


def _scatter_kernel(idx_ref, h_ref, y_ref, g_ref, o_ref, *, final_norm, norm_rows):
    b, e = pl.program_id(0), pl.program_id(1)
    ne = pl.num_programs(1)
    cap = y_ref.shape[0] * ROW_UNROLL
    base = (b * ne + e) * cap
    part = h_ref.shape[0]

    @pl.when(e == 0)
    def _():
        o_ref[...] = jnp.zeros(o_ref.shape, F32)

    r_lo = pl.multiple_of(e * part, part)
    o_ref[pl.ds(r_lo, part), :] = o_ref[pl.ds(r_lo, part), :] + h_ref[...]

    def add(i, carry):
        toks = [idx_ref[base + i * ROW_UNROLL + r] for r in range(ROW_UNROLL)]
        vals = [o_ref[pl.ds(toks[r], 1), :] + y_ref[i, r:r + 1, :] for r in range(ROW_UNROLL)]
        for r in range(ROW_UNROLL):
            o_ref[pl.ds(toks[r], 1), :] = vals[r]
        return carry

    lax.fori_loop(0, cap // ROW_UNROLL, add, 0)

    if final_norm:
        @pl.when(e == ne - 1)
        def _():
            g = g_ref[...]

            def norm(i, carry):
                r0 = pl.multiple_of(i * norm_rows, norm_rows)
                o_ref[pl.ds(r0, norm_rows), :] = _rms(o_ref[pl.ds(r0, norm_rows), :], g)
                return carry

            lax.fori_loop(0, o_ref.shape[0] // norm_rows, norm, 0)


def _scatter(idx_flat, h, y, g_final, *, ne, cap, final_norm):
    bsz, seq, d = h.shape
    y5 = y.reshape(ne, bsz, cap // ROW_UNROLL, ROW_UNROLL, d)
    part = seq // ne
    return pl.pallas_call(
        functools.partial(_scatter_kernel, final_norm=final_norm, norm_rows=min(seq, 256)),
        grid_spec=pltpu.PrefetchScalarGridSpec(
            num_scalar_prefetch=1,
            grid=(bsz, ne),
            in_specs=[
                pl.BlockSpec((None, part, d), lambda b, e, idx: (b, e, 0)),
                pl.BlockSpec((None, None, cap // ROW_UNROLL, ROW_UNROLL, d), lambda b, e, idx: (e, b, 0, 0, 0)),
                pl.BlockSpec((1, d), lambda b, e, idx: (0, 0)),
            ],
            out_specs=pl.BlockSpec((None, seq, d), lambda b, e, idx: (b, 0, 0)),
        ),
        out_shape=jax.ShapeDtypeStruct((bsz, seq, d), F32),
        compiler_params=_cparams("parallel", "arbitrary"),
        name="moe_scatter",
    )(idx_flat, h, y5, g_final.reshape(1, d))


def _moe_layer(h, g, w_router, w_gate, w_up, w_down, g_final, *, layer, final_norm, tiles):
    bsz, seq, d = h.shape
    ne = w_router.shape[1]
    cap = CAPACITY_FACTOR * seq // ne
    aff = _router(h, g, w_router, tile=tiles["router"])
    idx, gateb = _plan(aff, cap=cap)
    idx_flat = idx.reshape(-1)
    xg = _gather(idx_flat, h, g, ne=ne, cap=cap)
    y = _ffn(xg, gateb, w_gate, w_up, w_down, layer=layer, rows=tiles["ffn_rows"], fchunk=tiles["ffn_f"],
             sub=tiles["ffn_sub"])
    return _scatter(idx_flat, h, y, g_final, ne=ne, cap=cap, final_norm=final_norm)


def _tiles(bsz, seq, d, dff, cap):
    m = bsz * cap
    rows = min(m, 2048)
    rows = max(cap, rows // cap * cap)
    return {
        "conv": min(seq, 512),
        "qkv": min(seq, 512),
        "tq": min(seq, 512),
        "tk": min(seq, 512),
        "attn_rows": min(seq, 512),
        "oproj": min(seq, 1024),
        "router": min(seq, 1024),
        "ffn_rows": rows,
        "ffn_f": min(dff, 512),
        "ffn_sub": min(cap, 512),
    }


def kernel(x, norm_mix, norm_ffn, conv_in, conv_w, conv_out, attn_qkv, attn_q_norm, attn_k_norm, attn_out,
           router, w_gate, w_up, w_down, final_norm):
    bsz, seq, d = x.shape
    depth = norm_mix.shape[0]
    ne = router.shape[2]
    tiles = _tiles(bsz, seq, d, w_gate.shape[3], CAPACITY_FACTOR * seq // ne)
    h = x
    for i in range(depth):
        j = i // 2
        if i % 2 == 0:
            h = _conv_layer(h, norm_mix[i], conv_in[j], conv_w[j], conv_out[j], tile=tiles["conv"])
        else:
            hd = attn_q_norm.shape[1]
            q, k2, v2 = _qkv_layer(h, norm_mix[i], attn_qkv[j], attn_q_norm[j], attn_k_norm[j], tile=tiles["qkv"])
            o = _attention(q, k2, v2, hd=hd, tq=tiles["tq"], rows=tiles["attn_rows"], tk=tiles["tk"])
            h = _oproj_layer(o, h, attn_out[j], tile=tiles["oproj"])
        h = _moe_layer(h, norm_ffn[i], router[i], w_gate, w_up, w_down, final_norm,
                       layer=i, final_norm=(i == depth - 1), tiles=tiles)
    if depth == 0:
        raise NotImplementedError("depth 0")
    return h
```

```python
import functools

import jax
import jax.numpy as jnp
import numpy as np
from jax import lax
from jax.experimental import pallas as pl
from jax.experimental.pallas import tpu as pltpu

EPS = 1e-6
N_HEADS = 16
N_KV_HEADS = 4
GRID_W = 64
ROPE_THETA = 10000.0
CAPACITY_FACTOR = 2
LOG2E = 1.4426950408889634
ATTN_LOOKAHEAD = 3
ONES_ROWS = 16

LANES = 128
CUMSUM_CHUNK = 256
VMEM_LIMIT = 56 * 1024 * 1024

F32 = jnp.float32
BF16 = jnp.bfloat16


def _cparams(*sem):
    return pltpu.CompilerParams(dimension_semantics=sem, vmem_limit_bytes=VMEM_LIMIT)


def _rms(x, g):
    return x * lax.rsqrt(jnp.mean(x * x, axis=-1, keepdims=True) + EPS) * g


def _dot(a, b):
    return jnp.dot(a, b, preferred_element_type=F32)


def _dot_nt(a, b, precision=None):
    return lax.dot_general(a, b, (((1,), (1,)), ((), ())), precision=precision,
                           preferred_element_type=F32)


HALO = 8


def _conv_kernel(xp_ref, x_ref, xq_ref, g_ref, win_ref, wc_ref, wout_ref, o_ref, u_ref):
    i = pl.program_id(1)
    n = pl.num_programs(1)
    t = x_ref.shape[0]
    d = x_ref.shape[1]
    x = x_ref[...]
    g = g_ref[...]
    xs = jnp.concatenate([_rms(xp_ref[...], g), _rms(x, g), _rms(xq_ref[...], g)], axis=0).astype(BF16)
    c = _dot(xs, win_ref[:, d:2 * d])
    h = _dot(xs, win_ref[:, 2 * d:3 * d])
    u = c * h
    row = lax.broadcasted_iota(jnp.int32, (t + 2 * HALO, 1), 0)
    first_live = jnp.where(i == 0, HALO, 0)
    end_live = jnp.where(i == n - 1, HALO + t, t + 2 * HALO)
    u_ref[...] = jnp.where((row >= first_live) & (row < end_live), u, 0.0)
    wc = wc_ref[...]
    y = (wc[0:1, :] * u_ref[HALO - 1:HALO - 1 + t, :] + wc[1:2, :] * u_ref[HALO:HALO + t, :]
         + wc[2:3, :] * u_ref[HALO + 1:HALO + 1 + t, :])
    b = _dot(xs, win_ref[:, 0:d])[HALO:HALO + t, :]
    o_ref[...] = x + _dot((b * y).astype(BF16), wout_ref[...])


def _conv_layer(h, g, w_in, w_conv, w_out, *, tile):
    bsz, seq, d = h.shape
    tb = tile // HALO
    nb = seq // HALO
    return pl.pallas_call(
        _conv_kernel,
        grid=(bsz, seq // tile),
        in_specs=[
            pl.BlockSpec((None, HALO, d), lambda b, i: (b, jnp.maximum(i * tb - 1, 0), 0)),
            pl.BlockSpec((None, tile, d), lambda b, i: (b, i, 0)),
            pl.BlockSpec((None, HALO, d), lambda b, i: (b, jnp.minimum((i + 1) * tb, nb - 1), 0)),
            pl.BlockSpec((1, d), lambda b, i: (0, 0)),
            pl.BlockSpec((d, 3 * d), lambda b, i: (0, 0)),
            pl.BlockSpec((3, d), lambda b, i: (0, 0)),
            pl.BlockSpec((d, d), lambda b, i: (0, 0)),
        ],
        out_specs=pl.BlockSpec((None, tile, d), lambda b, i: (b, i, 0)),
        out_shape=jax.ShapeDtypeStruct((bsz, seq, d), F32),
        scratch_shapes=[pltpu.VMEM((tile + 2 * HALO, d), F32)],
        compiler_params=_cparams("parallel", "arbitrary"),
        name="conv_mixer",
    )(h, h, h, g.reshape(1, d), w_in.astype(BF16), w_conv, w_out.astype(BF16))


def _rope_tables(seq, head_dim, g, scale):
    axis = head_dim // 2
    quarter = axis // 2
    rows = seq // GRID_W
    row = jnp.repeat(jnp.arange(rows, dtype=F32), GRID_W)
    col = jnp.tile(jnp.arange(GRID_W, dtype=F32), rows)
    inv_freq = ROPE_THETA ** (-jnp.arange(0, axis, 2, dtype=F32) / axis)
    j = np.arange(head_dim)
    pos = jnp.where((j < axis)[None, :], row[:, None], col[:, None])
    ang = pos * inv_freq[j % quarter][None, :]
    first = (j % axis) < quarter
    partner = np.where(first, j + quarter, j - quarter)
    cos_t = jnp.cos(ang) * g[None, :] * scale
    sin_t = jnp.sin(ang) * jnp.where(first, -1.0, 1.0)[None, :] * g[partner][None, :] * scale
    return jnp.tile(cos_t, (1, 2)), jnp.tile(sin_t, (1, 2)), partner


def _qkv_kernel(x_ref, g_ref, w_ref, wvt_ref, cq_ref, sq_ref, ck_ref, sk_ref, q_ref, k_ref, vt_ref, *, hd):
    t, d = x_ref.shape
    nq = q_ref.shape[1]
    nk = k_ref.shape[1] // 2
    xn = _rms(x_ref[...], g_ref[...]).astype(BF16)
    lane = lax.broadcasted_iota(jnp.int32, (1, LANES), 1)
    lo = lane < hd

    def head_scale(raw):
        sq = raw * raw
        s0 = jnp.sum(jnp.where(lo, sq, 0.0), axis=-1, keepdims=True)
        s1 = jnp.sum(jnp.where(lo, 0.0, sq), axis=-1, keepdims=True)
        return jnp.where(lo, lax.rsqrt(s0 / hd + EPS), lax.rsqrt(s1 / hd + EPS))

    def dup(pair):
        sw = pltpu.roll(pair, hd, axis=1)
        return jnp.where(lo, pair, sw), jnp.where(lo, sw, pair)

    cq, sq_t = cq_ref[...], sq_ref[...]
    raw_q = _dot(xn, w_ref[:, 0:nq])
    swp_q = _dot(xn, w_ref[:, nq:2 * nq])
    for j in range(nq // LANES):
        raw = raw_q[:, j * LANES:(j + 1) * LANES]
        swp = swp_q[:, j * LANES:(j + 1) * LANES]
        q_ref[:, j * LANES:(j + 1) * LANES] = ((raw * cq + swp * sq_t) * head_scale(raw)).astype(BF16)
    ck, sk_t = ck_ref[...], sk_ref[...]
    raw_k = _dot(xn, w_ref[:, 2 * nq:2 * nq + nk])
    swp_k = _dot(xn, w_ref[:, 2 * nq + nk:2 * nq + 2 * nk])
    for j in range(nk // LANES):
        raw = raw_k[:, j * LANES:(j + 1) * LANES]
        swp = swp_k[:, j * LANES:(j + 1) * LANES]
        ka, kb = dup(((raw * ck + swp * sk_t) * head_scale(raw)).astype(F32))
        k_ref[:, 2 * j * LANES:(2 * j + 1) * LANES] = ka.astype(BF16)
        k_ref[:, (2 * j + 1) * LANES:(2 * j + 2) * LANES] = kb.astype(BF16)
    vt_ref[...] = _dot_nt(wvt_ref[...], xn).astype(BF16)


def _qkv_layer(h, g, w_qkv, g_q, g_k, *, tile):
    bsz, seq, d = h.shape
    hd = g_q.shape[0]
    assert 2 * hd == LANES
    nq, nk = N_HEADS * hd, N_KV_HEADS * hd
    cq, sq, partner = _rope_tables(seq, hd, g_q, hd ** -0.5 * LOG2E)
    ck, sk, _ = _rope_tables(seq, hd, g_k, 1.0)
    wq, wk, wv = w_qkv[:, :nq], w_qkv[:, nq:nq + nk], w_qkv[:, nq + nk:]
    perm_q = (np.arange(nq) // hd) * hd + partner[np.arange(nq) % hd]
    perm_k = (np.arange(nk) // hd) * hd + partner[np.arange(nk) % hd]
    w_all = jnp.concatenate([wq, wq[:, perm_q], wk, wk[:, perm_k]], axis=1).astype(BF16)
    wvt = wv.T.astype(BF16)
    ncol = w_all.shape[1]
    tab = pl.BlockSpec((tile, LANES), lambda b, i: (i, 0))
    return pl.pallas_call(
        functools.partial(_qkv_kernel, hd=hd),
        grid=(bsz, seq // tile),
        in_specs=[
            pl.BlockSpec((None, tile, d), lambda b, i: (b, i, 0)),
            pl.BlockSpec((1, d), lambda b, i: (0, 0)),
            pl.BlockSpec((d, ncol), lambda b, i: (0, 0)),
            pl.BlockSpec((nk, d), lambda b, i: (0, 0)),
            tab, tab, tab, tab,
        ],
        out_specs=[
            pl.BlockSpec((None, tile, nq), lambda b, i: (b, i, 0)),
            pl.BlockSpec((None, tile, 2 * nk), lambda b, i: (b, i, 0)),
            pl.BlockSpec((None, nk, tile), lambda b, i: (b, 0, i)),
        ],
        out_shape=[
            jax.ShapeDtypeStruct((bsz, seq, nq), BF16),
            jax.ShapeDtypeStruct((bsz, seq, 2 * nk), BF16),
            jax.ShapeDtypeStruct((bsz, nk, seq), BF16),
        ],
        compiler_params=_cparams("parallel", "arbitrary"),
        name="qkv_rope",
    )(h, g.reshape(1, d), w_all, wvt, cq, sq, ck, sk)


def _attn_kernel(q_ref, k_ref, vt_ref, o_ref, *, hd, rows, tk):
    nh = q_ref.shape[1] // hd
    tq = q_ref.shape[0]
    seq = k_ref.shape[0]
    lane = lax.broadcasted_iota(jnp.int32, (1, LANES), 1)
    lo = lane < hd
    zero = jnp.zeros((), BF16)
    ones = jnp.ones((ONES_ROWS, tk), BF16)

    for r in range(tq // rows):
        qs = []
        for h in range(nh):
            mine = lo if h % 2 == 0 else jnp.logical_not(lo)
            qs.append(jnp.where(mine, q_ref[r * rows:(r + 1) * rows, (h // 2) * LANES:(h // 2 + 1) * LANES], zero))

        stages = [(kc, h) for kc in range(seq // tk) for h in range(nh)]
        m = [jnp.full((1, rows), -jnp.inf, F32)] * nh
        acc = [jnp.zeros((hd + ONES_ROWS, rows), F32)] * nh
        scores, v_aug = {}, {}
        for i in range(len(stages) + ATTN_LOOKAHEAD):
            if i < len(stages):
                kc, h = stages[i]
                scores[i] = _dot_nt(k_ref[kc * tk:(kc + 1) * tk, :], qs[h])
            j = i - ATTN_LOOKAHEAD
            if j >= 0:
                kc, h = stages[j]
                if kc not in v_aug:
                    v_aug = {kc: jnp.concatenate([vt_ref[:, kc * tk:(kc + 1) * tk], ones], axis=0)}
                s = scores.pop(j)
                m_new = jnp.maximum(m[h], jnp.max(s, axis=0, keepdims=True))
                p = jnp.exp2(s - m_new)
                acc[h] = acc[h] * jnp.exp2(m[h] - m_new) + _dot(v_aug[kc], p.astype(BF16))
                m[h] = m_new
        for pair in range(nh // 2):
            halves = [acc[2 * pair + i][0:hd, :] / acc[2 * pair + i][hd:hd + 1, :] for i in range(2)]
            o_ref[r * rows:(r + 1) * rows, pair * LANES:(pair + 1) * LANES] = (
                jnp.concatenate(halves, axis=0).T.astype(o_ref.dtype))


def _attention(q, k2, vt, *, hd, tq, rows, tk):
    bsz, seq, nq = q.shape
    group = N_HEADS // N_KV_HEADS
    gw = group * hd
    return pl.pallas_call(
        functools.partial(_attn_kernel, hd=hd, rows=rows, tk=tk),
        grid=(bsz, N_KV_HEADS, seq // tq),
        in_specs=[
            pl.BlockSpec((None, tq, gw), lambda b, g, qi: (b, qi, g)),
            pl.BlockSpec((None, seq, LANES), lambda b, g, qi: (b, 0, g)),
            pl.BlockSpec((None, hd, seq), lambda b, g, qi: (b, g, 0)),
        ],
        out_specs=pl.BlockSpec((None, tq, gw), lambda b, g, qi: (b, qi, g)),
        out_shape=jax.ShapeDtypeStruct((bsz, seq, nq), BF16),
        compiler_params=_cparams("parallel", "parallel", "arbitrary"),
        name="flash_attn",
    )(q, k2, vt)


def _oproj_kernel(o_ref, h_ref, w_ref, out_ref):
    out_ref[...] = h_ref[...] + _dot(o_ref[...], w_ref[...])


def _oproj_layer(o, h, w_o, *, tile):
    bsz, seq, d = h.shape
    nq = o.shape[2]
    return pl.pallas_call(
        _oproj_kernel,
        grid=(bsz, seq // tile),
        in_specs=[
            pl.BlockSpec((None, tile, nq), lambda b, i: (b, i, 0)),
            pl.BlockSpec((None, tile, d), lambda b, i: (b, i, 0)),
            pl.BlockSpec((nq, d), lambda b, i: (0, 0)),
        ],
        out_specs=pl.BlockSpec((None, tile, d), lambda b, i: (b, i, 0)),
        out_shape=jax.ShapeDtypeStruct((bsz, seq, d), F32),
        compiler_params=_cparams("parallel", "arbitrary"),
        name="attn_out_proj",
    )(o, h, w_o.astype(BF16))


def _router_kernel(h_ref, g_ref, wh_ref, wl_ref, aff_ref):
    xn = _rms(h_ref[...], g_ref[...])
    xh = xn.astype(BF16)
    xl = (xn - xh.astype(F32)).astype(BF16)
    wh, wl = wh_ref[...], wl_ref[...]
    lg = (_dot_nt(wh, xh) + _dot_nt(wh, xl)) + (_dot_nt(wl, xh) + _dot_nt(wl, xl))
    ex = jnp.exp(lg - jnp.max(lg, axis=0, keepdims=True))
    aff_ref[...] = ex / jnp.sum(ex, axis=0, keepdims=True)


def _router(h, g, w_router, *, tile):
    bsz, seq, d = h.shape
    ne = w_router.shape[1]
    wt = w_router.T
    wt_hi = wt.astype(BF16)
    wt_lo = (wt - wt_hi.astype(F32)).astype(BF16)
    return pl.pallas_call(
        _router_kernel,
        grid=(bsz, seq // tile),
        in_specs=[
            pl.BlockSpec((None, tile, d), lambda b, i: (b, i, 0)),
            pl.BlockSpec((1, d), lambda b, i: (0, 0)),
            pl.BlockSpec((ne, d), lambda b, i: (0, 0)),
            pl.BlockSpec((ne, d), lambda b, i: (0, 0)),
        ],
        out_specs=pl.BlockSpec((None, ne, tile), lambda b, i: (b, 0, i)),
        out_shape=jax.ShapeDtypeStruct((bsz, ne, seq), F32),
        compiler_params=_cparams("parallel", "arbitrary"),
        name="router",
    )(h, g.reshape(1, d), wt_hi, wt_lo)


def _plan_kernel(aff_ref, tri_ref, pfx_ref, idx_ref, gate_ref, key_ref, affc_ref, filled_ref, *, cap, slot_block):
    ne, seq = aff_ref.shape
    nch = seq // CUMSUM_CHUNK
    a = aff_ref[...]

    def count(mask):
        return jnp.sum(jnp.where(mask, 1.0, 0.0), axis=1, keepdims=True)

    def search(j, thr_bits):
        cand = thr_bits | lax.shift_left(jnp.int32(1), 30 - j)
        return jnp.where(count(a >= pltpu.bitcast(cand, F32)) >= cap, cand, thr_bits)

    thr = pltpu.bitcast(lax.fori_loop(0, 31, search, jnp.zeros((ne, 1), jnp.int32)), F32)

    def cumsum(m):
        st = jnp.concatenate([m[:, c * CUMSUM_CHUNK:(c + 1) * CUMSUM_CHUNK] for c in range(nch)], axis=0)
        within = _dot(st.astype(BF16), tri_ref[...])
        tot = jnp.broadcast_to(within[:, CUMSUM_CHUNK - 1:CUMSUM_CHUNK], (nch * ne, LANES))
        offs = _dot(pfx_ref[...], tot.astype(BF16))
        full = within + offs[:, 0:1]
        return jnp.concatenate([full[c * ne:(c + 1) * ne, :] for c in range(nch)], axis=1)

    gt = a > thr
    eq = a == thr
    need = cap - count(gt)
    eq_rank = cumsum(jnp.where(eq, 1.0, 0.0))
    sel = gt | (eq & (eq_rank <= need))
    rank = cumsum(jnp.where(sel, 1.0, 0.0))
    key = jnp.where(sel, rank, 0.0)

    nck = seq // LANES
    lane_i = lax.broadcasted_iota(jnp.int32, (1, LANES), 1)
    filled = jnp.full((ne, LANES), 1e9, F32)
    for c in range(nck):
        key_ref[c] = key[:, c * LANES:(c + 1) * LANES]
        affc_ref[c] = a[:, c * LANES:(c + 1) * LANES]
        filled = jnp.where(lane_i == c, jnp.max(rank[:, c * LANES:(c + 1) * LANES], axis=1, keepdims=True), filled)
    filled_ref[...] = filled

    nblk = cap // slot_block
    slot1 = (lax.broadcasted_iota(jnp.int32, (slot_block, LANES), 0) + 1).astype(F32)
    lane = lane_i.astype(F32)
    ones = jnp.ones((8, LANES), F32)

    def scalar_count(mask):
        return jnp.sum(jnp.where(mask, 1.0, 0.0), axis=1, keepdims=True).astype(jnp.int32)[0, 0]

    def per_expert(e, carry):
        filled_row = filled_ref[pl.ds(e, 1), :]
        bounds = [(scalar_count(filled_row <= float(pb * slot_block)),
                   jnp.minimum(scalar_count(filled_row < float((pb + 1) * slot_block)) + 1, nck))
                  for pb in range(nblk)]
        accs_t, accs_g = [], []
        for pb in range(nblk):
            want = slot1 + float(pb * slot_block)

            def chunk(c, accs, want=want):
                hit = key_ref[c, pl.ds(e, 1), :] == want
                tok = lane + (c * LANES).astype(F32)
                return (accs[0] + jnp.where(hit, tok, 0.0),
                        accs[1] + jnp.where(hit, affc_ref[c, pl.ds(e, 1), :], 0.0))

            zeros = jnp.zeros((slot_block, LANES), F32)
            acc_t, acc_g = lax.fori_loop(bounds[pb][0], bounds[pb][1], chunk, (zeros, zeros))
            accs_t.append(acc_t)
            accs_g.append(acc_g)
        tok = _dot_nt(ones, jnp.concatenate(accs_t, axis=0), precision=lax.Precision.HIGHEST)
        idx_ref[e] = tok[0:1, :].astype(jnp.int32)
        gate = jnp.sum(jnp.concatenate(accs_g, axis=0), axis=1, keepdims=True)
        gate_ref[e] = jnp.broadcast_to(gate, (cap, LANES))
        return carry

    lax.fori_loop(0, ne, per_expert, 0)


def _plan(aff, *, cap):
    bsz, ne, seq = aff.shape
    nch = seq // CUMSUM_CHUNK
    r = np.arange(CUMSUM_CHUNK)
    tri = jnp.asarray(r[:, None] <= r[None, :], BF16)
    q = np.arange(nch * ne)
    pfx = jnp.asarray((q[:, None] % ne == q[None, :] % ne) & (q[None, :] // ne < q[:, None] // ne), BF16)
    slot_block = min(cap, LANES)
    return pl.pallas_call(
        functools.partial(_plan_kernel, cap=cap, slot_block=slot_block),
        grid=(bsz,),
        in_specs=[
            pl.BlockSpec((None, ne, seq), lambda b: (b, 0, 0)),
            pl.BlockSpec((CUMSUM_CHUNK, CUMSUM_CHUNK), lambda b: (0, 0)),
            pl.BlockSpec((nch * ne, nch * ne), lambda b: (0, 0)),
        ],
        out_specs=[
            pl.BlockSpec((None, ne, 1, cap), lambda b: (b, 0, 0, 0)),
            pl.BlockSpec((None, ne, cap, LANES), lambda b: (b, 0, 0, 0)),
        ],
        out_shape=[
            jax.ShapeDtypeStruct((bsz, ne, 1, cap), jnp.int32),
            jax.ShapeDtypeStruct((bsz, ne, cap, LANES), F32),
        ],
        scratch_shapes=[
            pltpu.VMEM((seq // LANES, ne, LANES), F32),
            pltpu.VMEM((seq // LANES, ne, LANES), F32),
            pltpu.VMEM((ne, LANES), F32),
        ],
        compiler_params=_cparams("parallel"),
        name="route_plan",
    )(aff, tri, pfx)


ROW_UNROLL = 8
TILES_PER_ITER = 4


def _gather_kernel(idx_ref, h_ref, g_ref, xg_ref, rows_ref):
    b, e = pl.program_id(0), pl.program_id(1)
    cap, d = xg_ref.shape
    base = (b * pl.num_programs(1) + e) * cap

    def move(i, carry):
        for u in range(TILES_PER_ITER):
            t = i * TILES_PER_ITER + u
            for r in range(ROW_UNROLL):
                rows_ref[t, r:r + 1, :] = h_ref[pl.ds(idx_ref[base + t * ROW_UNROLL + r], 1), :]
        return carry

    lax.fori_loop(0, cap // (ROW_UNROLL * TILES_PER_ITER), move, 0)
    xg_ref[...] = _rms(rows_ref[...].reshape(cap, d), g_ref[...]).astype(BF16)


def _gather(idx_flat, h, g, *, ne, cap):
    bsz, seq, d = h.shape
    return pl.pallas_call(
        _gather_kernel,
        grid_spec=pltpu.PrefetchScalarGridSpec(
            num_scalar_prefetch=1,
            grid=(bsz, ne),
            in_specs=[
                pl.BlockSpec((None, seq, d), lambda b, e, idx: (b, 0, 0)),
                pl.BlockSpec((1, d), lambda b, e, idx: (0, 0)),
            ],
            out_specs=pl.BlockSpec((None, None, cap, d), lambda b, e, idx: (e, b, 0, 0)),
            scratch_shapes=[pltpu.VMEM((cap // ROW_UNROLL, ROW_UNROLL, d), F32)],
        ),
        out_shape=jax.ShapeDtypeStruct((ne, bsz, cap, d), BF16),
        compiler_params=_cparams("parallel", "arbitrary"),
        name="moe_gather",
    )(idx_flat, h, g.reshape(1, d))


def _ffn_kernel(x_ref, gate_ref, wg_ref, wu_ref, wd_ref, y_ref, wgb_ref, wub_ref, wdb_ref, *, sub):
    f = pl.program_id(2)
    nf = pl.num_programs(2)
    wgb_ref[...] = wg_ref[...].astype(BF16)
    wub_ref[...] = wu_ref[...].astype(BF16)
    wdb_ref[...] = wd_ref[...].astype(BF16)

    @pl.when(f == 0)
    def _():
        y_ref[...] = jnp.zeros(y_ref.shape, F32)

    nb, cap = gate_ref.shape[0], gate_ref.shape[1]
    for bi in range(nb):
        for s in range(cap // sub):
            r0 = bi * cap + s * sub
            x = x_ref[r0:r0 + sub, :]
            h1 = _dot(x, wgb_ref[...])
            h2 = _dot(x, wub_ref[...])
            act = (h1 * jax.nn.sigmoid(h1) * h2).astype(BF16)
            y = y_ref[r0:r0 + sub, :] + _dot(act, wdb_ref[...])
            gate = gate_ref[bi, s * sub:(s + 1) * sub, 0:1]
            y_ref[r0:r0 + sub, :] = jnp.where(f == nf - 1, y * gate, y)


def _ffn(xg, gateb, w_gate, w_up, w_down, *, layer, rows, fchunk, sub):
    ne, bsz, cap, d = xg.shape
    m = bsz * cap
    dff = w_gate.shape[3]
    nb = rows // cap
    x2 = xg.reshape(ne, m, d)
    return pl.pallas_call(
        functools.partial(_ffn_kernel, sub=sub),
        grid=(ne, m // rows, dff // fchunk),
        in_specs=[
            pl.BlockSpec((None, rows, d), lambda e, r, f: (e, r, 0)),
            pl.BlockSpec((nb, None, cap, LANES), lambda e, r, f: (r, e, 0, 0)),
            pl.BlockSpec((None, None, d, fchunk), lambda e, r, f: (layer, e, 0, f)),
            pl.BlockSpec((None, None, d, fchunk), lambda e, r, f: (layer, e, 0, f)),
            pl.BlockSpec((None, None, fchunk, d), lambda e, r, f: (layer, e, f, 0)),
        ],
        out_specs=pl.BlockSpec((None, rows, d), lambda e, r, f: (e, r, 0)),
        out_shape=jax.ShapeDtypeStruct((ne, m, d), F32),
        scratch_shapes=[
            pltpu.VMEM((d, fchunk), BF16),
            pltpu.VMEM((d, fchunk), BF16),
            pltpu.VMEM((fchunk, d), BF16),
        ],
        compiler_params=_cparams("parallel", "parallel", "arbitrary"),
        name="moe_experts",
    )(x2, gateb, w_gate, w_up, w_down)


def _scatter_kernel(idx_ref, h_ref, y_ref, g_ref, o_ref, *, final_norm, norm_rows):
    b, e = pl.program_id(0), pl.program_id(1)
    ne = pl.num_programs(1)
    cap = y_ref.shape[0] * ROW_UNROLL
    base = (b * ne + e) * cap
    part = h_ref.shape[0]

    @pl.when(e == 0)
    def _():
        o_ref[...] = jnp.zeros(o_ref.shape, F32)

    r_lo = pl.multiple_of(e * part, part)
    o_ref[pl.ds(r_lo, part), :] = o_ref[pl.ds(r_lo, part), :] + h_ref[...]

    def add(i, carry):
        for u in range(TILES_PER_ITER):
            t = i * TILES_PER_ITER + u
            toks = [idx_ref[base + t * ROW_UNROLL + r] for r in range(ROW_UNROLL)]
            vals = [o_ref[pl.ds(toks[r], 1), :] + y_ref[t, r:r + 1, :] for r in range(ROW_UNROLL)]
            for r in range(ROW_UNROLL):
                o_ref[pl.ds(toks[r], 1), :] = vals[r]
        return carry

    lax.fori_loop(0, cap // (ROW_UNROLL * TILES_PER_ITER), add, 0)

    if final_norm:
        @pl.when(e == ne - 1)
        def _():
            g = g_ref[...]

            def norm(i, carry):
                r0 = pl.multiple_of(i * norm_rows, norm_rows)
                o_ref[pl.ds(r0, norm_rows), :] = _rms(o_ref[pl.ds(r0, norm_rows), :], g)
                return carry

            lax.fori_loop(0, o_ref.shape[0] // norm_rows, norm, 0)


def _scatter(idx_flat, h, y, g_final, *, ne, cap, final_norm):
    bsz, seq, d = h.shape
    y5 = y.reshape(ne, bsz, cap // ROW_UNROLL, ROW_UNROLL, d)
    part = seq // ne
    return pl.pallas_call(
        functools.partial(_scatter_kernel, final_norm=final_norm, norm_rows=min(seq, 256)),
        grid_spec=pltpu.PrefetchScalarGridSpec(
            num_scalar_prefetch=1,
            grid=(bsz, ne),
            in_specs=[
                pl.BlockSpec((None, part, d), lambda b, e, idx: (b, e, 0)),
                pl.BlockSpec((None, None, cap // ROW_UNROLL, ROW_UNROLL, d), lambda b, e, idx: (e, b, 0, 0, 0)),
                pl.BlockSpec((1, d), lambda b, e, idx: (0, 0)),
            ],
            out_specs=pl.BlockSpec((None, seq, d), lambda b, e, idx: (b, 0, 0)),
        ),
        out_shape=jax.ShapeDtypeStruct((bsz, seq, d), F32),
        compiler_params=_cparams("parallel", "arbitrary"),
        name="moe_scatter",
    )(idx_flat, h, y5, g_final.reshape(1, d))


def _moe_layer(h, g, w_router, w_gate, w_up, w_down, g_final, *, layer, final_norm, tiles):
    bsz, seq, d = h.shape
    ne = w_router.shape[1]
    cap = CAPACITY_FACTOR * seq // ne
    aff = _router(h, g, w_router, tile=tiles["router"])
    idx, gateb = _plan(aff, cap=cap)
    idx_flat = idx.reshape(-1)
    xg = _gather(idx_flat, h, g, ne=ne, cap=cap)
    y = _ffn(xg, gateb, w_gate, w_up, w_down, layer=layer, rows=tiles["ffn_rows"], fchunk=tiles["ffn_f"],
             sub=tiles["ffn_sub"])
    return _scatter(idx_flat, h, y, g_final, ne=ne, cap=cap, final_norm=final_norm)


def _tiles(bsz, seq, d, dff, cap):
    m = bsz * cap
    rows = min(m, 2048)
    rows = max(cap, rows // cap * cap)
    return {
        "conv": min(seq, 512),
        "qkv": min(seq, 512),
        "tq": min(seq, 512),
        "tk": min(seq, 512),
        "attn_rows": min(seq, 256),
        "oproj": min(seq, 1024),
        "router": min(seq, 1024),
        "ffn_rows": rows,
        "ffn_f": min(dff, 512),
        "ffn_sub": min(cap, 512),
    }


def kernel(x, norm_mix, norm_ffn, conv_in, conv_w, conv_out, attn_qkv, attn_q_norm, attn_k_norm, attn_out,
           router, w_gate, w_up, w_down, final_norm):
    bsz, seq, d = x.shape
    depth = norm_mix.shape[0]
    ne = router.shape[2]
    tiles = _tiles(bsz, seq, d, w_gate.shape[3], CAPACITY_FACTOR * seq // ne)
    h = x
    for i in range(depth):
        j = i // 2
        if i % 2 == 0:
            h = _conv_layer(h, norm_mix[i], conv_in[j], conv_w[j], conv_out[j], tile=tiles["conv"])
        else:
            hd = attn_q_norm.shape[1]
            q, k2, v2 = _qkv_layer(h, norm_mix[i], attn_qkv[j], attn_q_norm[j], attn_k_norm[j], tile=tiles["qkv"])
            o = _attention(q, k2, v2, hd=hd, tq=tiles["tq"], rows=tiles["attn_rows"], tk=tiles["tk"])
            h = _oproj_layer(o, h, attn_out[j], tile=tiles["oproj"])
        h = _moe_layer(h, norm_ffn[i], router[i], w_gate, w_up, w_down, final_norm,
                       layer=i, final_norm=(i == depth - 1), tiles=tiles)
    if depth == 0:
        raise NotImplementedError("depth 0")
    return h
```

```python
import functools

import jax
import jax.numpy as jnp
import numpy as np
from jax import lax
from jax.experimental import pallas as pl
from jax.experimental.pallas import tpu as pltpu

EPS = 1e-6
N_HEADS = 16
N_KV_HEADS = 4
GRID_W = 64
ROPE_THETA = 10000.0
CAPACITY_FACTOR = 2
LOG2E = 1.4426950408889634
ATTN_LOOKAHEAD = 3
ONES_ROWS = 16

LANES = 128
CUMSUM_CHUNK = 256
VMEM_LIMIT = 56 * 1024 * 1024

F32 = jnp.float32
BF16 = jnp.bfloat16


def _cparams(*sem):
    return pltpu.CompilerParams(dimension_semantics=sem, vmem_limit_bytes=VMEM_LIMIT)


def _rms(x, g):
    return x * lax.rsqrt(jnp.mean(x * x, axis=-1, keepdims=True) + EPS) * g


def _dot(a, b):
    return jnp.dot(a, b, preferred_element_type=F32)


def _dot_nt(a, b, precision=None):
    return lax.dot_general(a, b, (((1,), (1,)), ((), ())), precision=precision,
                           preferred_element_type=F32)


def _affinities(h, g, w_split):
    ne = w_split.shape[0] // 2
    xn = _rms(h, g)
    xh = xn.astype(BF16)
    xl = (xn - xh.astype(F32)).astype(BF16)
    part = _dot_nt(w_split, xh) + _dot_nt(w_split, xl)
    lg = part[0:ne, :] + part[ne:2 * ne, :]
    ex = jnp.exp(lg - jnp.max(lg, axis=0, keepdims=True))
    return ex / jnp.sum(ex, axis=0, keepdims=True)


def _router_operands(g, w_router):
    wt = w_router.T
    wt_hi = wt.astype(BF16)
    wt_lo = (wt - wt_hi.astype(F32)).astype(BF16)
    return g.reshape(1, -1), jnp.concatenate([wt_hi, wt_lo], axis=0)


def _router_specs(ne, d):
    return [pl.BlockSpec((1, d), lambda b, i: (0, 0)),
            pl.BlockSpec((2 * ne, d), lambda b, i: (0, 0))]


HALO = 8


def _conv_kernel(xp_ref, x_ref, xq_ref, g_ref, win_ref, wc_ref, wout_ref, gr_ref, wr_ref, o_ref, aff_ref, u_ref):
    i = pl.program_id(1)
    n = pl.num_programs(1)
    t = x_ref.shape[0]
    d = x_ref.shape[1]
    x = x_ref[...]
    g = g_ref[...]
    xs = jnp.concatenate([_rms(xp_ref[...], g), _rms(x, g), _rms(xq_ref[...], g)], axis=0).astype(BF16)
    c = _dot(xs, win_ref[:, d:2 * d])
    h = _dot(xs, win_ref[:, 2 * d:3 * d])
    u = c * h
    row = lax.broadcasted_iota(jnp.int32, (t + 2 * HALO, 1), 0)
    first_live = jnp.where(i == 0, HALO, 0)
    end_live = jnp.where(i == n - 1, HALO + t, t + 2 * HALO)
    u_ref[...] = jnp.where((row >= first_live) & (row < end_live), u, 0.0)
    wc = wc_ref[...]
    y = (wc[0:1, :] * u_ref[HALO - 1:HALO - 1 + t, :] + wc[1:2, :] * u_ref[HALO:HALO + t, :]
         + wc[2:3, :] * u_ref[HALO + 1:HALO + 1 + t, :])
    b = _dot(xs, win_ref[:, 0:d])[HALO:HALO + t, :]
    out = x + _dot((b * y).astype(BF16), wout_ref[...])
    o_ref[...] = out
    aff_ref[...] = _affinities(out, gr_ref[...], wr_ref[...])


def _conv_layer(h, g, w_in, w_conv, w_out, g_ffn, w_router, *, tile):
    bsz, seq, d = h.shape
    ne = w_router.shape[1]
    tb = tile // HALO
    nb = seq // HALO
    return pl.pallas_call(
        _conv_kernel,
        grid=(bsz, seq // tile),
        in_specs=[
            pl.BlockSpec((None, HALO, d), lambda b, i: (b, jnp.maximum(i * tb - 1, 0), 0)),
            pl.BlockSpec((None, tile, d), lambda b, i: (b, i, 0)),
            pl.BlockSpec((None, HALO, d), lambda b, i: (b, jnp.minimum((i + 1) * tb, nb - 1), 0)),
            pl.BlockSpec((1, d), lambda b, i: (0, 0)),
            pl.BlockSpec((d, 3 * d), lambda b, i: (0, 0)),
            pl.BlockSpec((3, d), lambda b, i: (0, 0)),
            pl.BlockSpec((d, d), lambda b, i: (0, 0)),
        ] + _router_specs(ne, d),
        out_specs=[
            pl.BlockSpec((None, tile, d), lambda b, i: (b, i, 0)),
            pl.BlockSpec((None, ne, tile), lambda b, i: (b, 0, i)),
        ],
        out_shape=[
            jax.ShapeDtypeStruct((bsz, seq, d), F32),
            jax.ShapeDtypeStruct((bsz, ne, seq), F32),
        ],
        scratch_shapes=[pltpu.VMEM((tile + 2 * HALO, d), F32)],
        compiler_params=_cparams("parallel", "arbitrary"),
        name="conv_mixer",
    )(h, h, h, g.reshape(1, d), w_in.astype(BF16), w_conv, w_out.astype(BF16), *_router_operands(g_ffn, w_router))


def _rope_tables(seq, head_dim, g, scale):
    axis = head_dim // 2
    quarter = axis // 2
    rows = seq // GRID_W
    row = jnp.repeat(jnp.arange(rows, dtype=F32), GRID_W)
    col = jnp.tile(jnp.arange(GRID_W, dtype=F32), rows)
    inv_freq = ROPE_THETA ** (-jnp.arange(0, axis, 2, dtype=F32) / axis)
    j = np.arange(head_dim)
    pos = jnp.where((j < axis)[None, :], row[:, None], col[:, None])
    ang = pos * inv_freq[j % quarter][None, :]
    first = (j % axis) < quarter
    partner = np.where(first, j + quarter, j - quarter)
    cos_t = jnp.cos(ang) * g[None, :] * scale
    sin_t = jnp.sin(ang) * jnp.where(first, -1.0, 1.0)[None, :] * g[partner][None, :] * scale
    return jnp.tile(cos_t, (1, 2)), jnp.tile(sin_t, (1, 2)), partner


def _qkv_kernel(x_ref, g_ref, w_ref, wvt_ref, cq_ref, sq_ref, ck_ref, sk_ref, q_ref, k_ref, vt_ref, *, hd):
    t, d = x_ref.shape
    nq = q_ref.shape[1]
    nk = k_ref.shape[1] // 2
    xn = _rms(x_ref[...], g_ref[...]).astype(BF16)
    lane = lax.broadcasted_iota(jnp.int32, (1, LANES), 1)
    lo = lane < hd

    def head_scale(raw):
        sq = raw * raw
        s0 = jnp.sum(jnp.where(lo, sq, 0.0), axis=-1, keepdims=True)
        s1 = jnp.sum(jnp.where(lo, 0.0, sq), axis=-1, keepdims=True)
        return jnp.where(lo, lax.rsqrt(s0 / hd + EPS), lax.rsqrt(s1 / hd + EPS))

    def dup(pair):
        sw = pltpu.roll(pair, hd, axis=1)
        return jnp.where(lo, pair, sw), jnp.where(lo, sw, pair)

    cq, sq_t = cq_ref[...], sq_ref[...]
    raw_q = _dot(xn, w_ref[:, 0:nq])
    swp_q = _dot(xn, w_ref[:, nq:2 * nq])
    for j in range(nq // LANES):
        raw = raw_q[:, j * LANES:(j + 1) * LANES]
        swp = swp_q[:, j * LANES:(j + 1) * LANES]
        q_ref[:, j * LANES:(j + 1) * LANES] = ((raw * cq + swp * sq_t) * head_scale(raw)).astype(BF16)
    ck, sk_t = ck_ref[...], sk_ref[...]
    raw_k = _dot(xn, w_ref[:, 2 * nq:2 * nq + nk])
    swp_k = _dot(xn, w_ref[:, 2 * nq + nk:2 * nq + 2 * nk])
    for j in range(nk // LANES):
        raw = raw_k[:, j * LANES:(j + 1) * LANES]
        swp = swp_k[:, j * LANES:(j + 1) * LANES]
        ka, kb = dup(((raw * ck + swp * sk_t) * head_scale(raw)).astype(F32))
        k_ref[:, 2 * j * LANES:(2 * j + 1) * LANES] = ka.astype(BF16)
        k_ref[:, (2 * j + 1) * LANES:(2 * j + 2) * LANES] = kb.astype(BF16)
    vt_ref[...] = _dot_nt(wvt_ref[...], xn).astype(BF16)


def _qkv_layer(h, g, w_qkv, g_q, g_k, *, tile):
    bsz, seq, d = h.shape
    hd = g_q.shape[0]
    assert 2 * hd == LANES
    nq, nk = N_HEADS * hd, N_KV_HEADS * hd
    cq, sq, partner = _rope_tables(seq, hd, g_q, hd ** -0.5 * LOG2E)
    ck, sk, _ = _rope_tables(seq, hd, g_k, 1.0)
    wq, wk, wv = w_qkv[:, :nq], w_qkv[:, nq:nq + nk], w_qkv[:, nq + nk:]
    perm_q = (np.arange(nq) // hd) * hd + partner[np.arange(nq) % hd]
    perm_k = (np.arange(nk) // hd) * hd + partner[np.arange(nk) % hd]
    w_all = jnp.concatenate([wq, wq[:, perm_q], wk, wk[:, perm_k]], axis=1).astype(BF16)
    wvt = wv.T.astype(BF16)
    ncol = w_all.shape[1]
    tab = pl.BlockSpec((tile, LANES), lambda b, i: (i, 0))
    return pl.pallas_call(
        functools.partial(_qkv_kernel, hd=hd),
        grid=(bsz, seq // tile),
        in_specs=[
            pl.BlockSpec((None, tile, d), lambda b, i: (b, i, 0)),
            pl.BlockSpec((1, d), lambda b, i: (0, 0)),
            pl.BlockSpec((d, ncol), lambda b, i: (0, 0)),
            pl.BlockSpec((nk, d), lambda b, i: (0, 0)),
            tab, tab, tab, tab,
        ],
        out_specs=[
            pl.BlockSpec((None, tile, nq), lambda b, i: (b, i, 0)),
            pl.BlockSpec((None, tile, 2 * nk), lambda b, i: (b, i, 0)),
            pl.BlockSpec((None, nk, tile), lambda b, i: (b, 0, i)),
        ],
        out_shape=[
            jax.ShapeDtypeStruct((bsz, seq, nq), BF16),
            jax.ShapeDtypeStruct((bsz, seq, 2 * nk), BF16),
            jax.ShapeDtypeStruct((bsz, nk, seq), BF16),
        ],
        compiler_params=_cparams("parallel", "arbitrary"),
        name="qkv_rope",
    )(h, g.reshape(1, d), w_all, wvt, cq, sq, ck, sk)


def _attn_kernel(q_ref, k_ref, vt_ref, o_ref, *, hd, rows, tk):
    nh = q_ref.shape[1] // hd
    tq = q_ref.shape[0]
    seq = k_ref.shape[0]
    lane = lax.broadcasted_iota(jnp.int32, (1, LANES), 1)
    lo = lane < hd
    zero = jnp.zeros((), BF16)
    ones = jnp.ones((ONES_ROWS, tk), BF16)

    for r in range(tq // rows):
        qs = []
        for h in range(nh):
            mine = lo if h % 2 == 0 else jnp.logical_not(lo)
            qs.append(jnp.where(mine, q_ref[r * rows:(r + 1) * rows, (h // 2) * LANES:(h // 2 + 1) * LANES], zero))

        stages = [(kc, h) for kc in range(seq // tk) for h in range(nh)]
        m = [jnp.full((1, rows), -jnp.inf, F32)] * nh
        acc = [jnp.zeros((hd + ONES_ROWS, rows), F32)] * nh
        scores, v_aug = {}, {}
        for i in range(len(stages) + ATTN_LOOKAHEAD):
            if i < len(stages):
                kc, h = stages[i]
                scores[i] = _dot_nt(k_ref[kc * tk:(kc + 1) * tk, :], qs[h])
            j = i - ATTN_LOOKAHEAD
            if j >= 0:
                kc, h = stages[j]
                if kc not in v_aug:
                    v_aug = {kc: jnp.concatenate([vt_ref[:, kc * tk:(kc + 1) * tk], ones], axis=0)}
                s = scores.pop(j)
                m_new = jnp.maximum(m[h], jnp.max(s, axis=0, keepdims=True))
                p = jnp.exp2(s - m_new)
                acc[h] = acc[h] * jnp.exp2(m[h] - m_new) + _dot(v_aug[kc], p.astype(BF16))
                m[h] = m_new
        for pair in range(nh // 2):
            halves = [acc[2 * pair + i][0:hd, :] / acc[2 * pair + i][hd:hd + 1, :] for i in range(2)]
            o_ref[r * rows:(r + 1) * rows, pair * LANES:(pair + 1) * LANES] = (
                jnp.concatenate(halves, axis=0).T.astype(o_ref.dtype))


def _attention(q, k2, vt, *, hd, tq, rows, tk):
    bsz, seq, nq = q.shape
    group = N_HEADS // N_KV_HEADS
    gw = group * hd
    return pl.pallas_call(
        functools.partial(_attn_kernel, hd=hd, rows=rows, tk=tk),
        grid=(bsz, N_KV_HEADS, seq // tq),
        in_specs=[
            pl.BlockSpec((None, tq, gw), lambda b, g, qi: (b, qi, g)),
            pl.BlockSpec((None, seq, LANES), lambda b, g, qi: (b, 0, g)),
            pl.BlockSpec((None, hd, seq), lambda b, g, qi: (b, g, 0)),
        ],
        out_specs=pl.BlockSpec((None, tq, gw), lambda b, g, qi: (b, qi, g)),
        out_shape=jax.ShapeDtypeStruct((bsz, seq, nq), BF16),
        compiler_params=_cparams("parallel", "parallel", "arbitrary"),
        name="flash_attn",
    )(q, k2, vt)


def _oproj_kernel(o_ref, h_ref, w_ref, gr_ref, wr_ref, out_ref, aff_ref):
    out = h_ref[...] + _dot(o_ref[...], w_ref[...])
    out_ref[...] = out
    aff_ref[...] = _affinities(out, gr_ref[...], wr_ref[...])


def _oproj_layer(o, h, w_o, g_ffn, w_router, *, tile):
    bsz, seq, d = h.shape
    nq = o.shape[2]
    ne = w_router.shape[1]
    return pl.pallas_call(
        _oproj_kernel,
        grid=(bsz, seq // tile),
        in_specs=[
            pl.BlockSpec((None, tile, nq), lambda b, i: (b, i, 0)),
            pl.BlockSpec((None, tile, d), lambda b, i: (b, i, 0)),
            pl.BlockSpec((nq, d), lambda b, i: (0, 0)),
        ] + _router_specs(ne, d),
        out_specs=[
            pl.BlockSpec((None, tile, d), lambda b, i: (b, i, 0)),
            pl.BlockSpec((None, ne, tile), lambda b, i: (b, 0, i)),
        ],
        out_shape=[
            jax.ShapeDtypeStruct((bsz, seq, d), F32),
            jax.ShapeDtypeStruct((bsz, ne, seq), F32),
        ],
        compiler_params=_cparams("parallel", "arbitrary"),
        name="attn_out_proj",
    )(o, h, w_o.astype(BF16), *_router_operands(g_ffn, w_router))


def _plan_kernel(aff_ref, tri_ref, pfx_ref, idx_ref, gate_ref, key_ref, affc_ref, filled_ref, *, cap, slot_block):
    ne, seq = aff_ref.shape
    nch = seq // CUMSUM_CHUNK
    a = aff_ref[...]

    def count(mask):
        return jnp.sum(jnp.where(mask, 1.0, 0.0), axis=1, keepdims=True)

    def search(j, thr_bits):
        cand = thr_bits | lax.shift_left(jnp.int32(1), 30 - j)
        return jnp.where(count(a >= pltpu.bitcast(cand, F32)) >= cap, cand, thr_bits)

    thr = pltpu.bitcast(lax.fori_loop(0, 31, search, jnp.zeros((ne, 1), jnp.int32)), F32)

    def cumsum(m):
        st = jnp.concatenate([m[:, c * CUMSUM_CHUNK:(c + 1) * CUMSUM_CHUNK] for c in range(nch)], axis=0)
        within = _dot(st.astype(BF16), tri_ref[...])
        tot = jnp.broadcast_to(within[:, CUMSUM_CHUNK - 1:CUMSUM_CHUNK], (nch * ne, LANES))
        offs = _dot(pfx_ref[...], tot.astype(BF16))
        full = within + offs[:, 0:1]
        return jnp.concatenate([full[c * ne:(c + 1) * ne, :] for c in range(nch)], axis=1)

    gt = a > thr
    eq = a == thr
    need = cap - count(gt)
    eq_rank = cumsum(jnp.where(eq, 1.0, 0.0))
    sel = gt | (eq & (eq_rank <= need))
    rank = cumsum(jnp.where(sel, 1.0, 0.0))
    key = jnp.where(sel, rank, 0.0)

    nck = seq // LANES
    lane_i = lax.broadcasted_iota(jnp.int32, (1, LANES), 1)
    filled = jnp.full((ne, LANES), 1e9, F32)
    for c in range(nck):
        key_ref[c] = key[:, c * LANES:(c + 1) * LANES]
        affc_ref[c] = a[:, c * LANES:(c + 1) * LANES]
        filled = jnp.where(lane_i == c, jnp.max(rank[:, c * LANES:(c + 1) * LANES], axis=1, keepdims=True), filled)
    filled_ref[...] = filled

    nblk = cap // slot_block
    slot1 = (lax.broadcasted_iota(jnp.int32, (slot_block, LANES), 0) + 1).astype(F32)
    lane = lane_i.astype(F32)
    ones = jnp.ones((8, LANES), F32)

    def scalar_count(mask):
        return jnp.sum(jnp.where(mask, 1.0, 0.0), axis=1, keepdims=True).astype(jnp.int32)[0, 0]

    def per_expert(e, carry):
        filled_row = filled_ref[pl.ds(e, 1), :]
        bounds = [(scalar_count(filled_row <= float(pb * slot_block)),
                   jnp.minimum(scalar_count(filled_row < float((pb + 1) * slot_block)) + 1, nck))
                  for pb in range(nblk)]
        accs_t, accs_g = [], []
        for pb in range(nblk):
            want = slot1 + float(pb * slot_block)

            def chunk(c, accs, want=want):
                hit = key_ref[c, pl.ds(e, 1), :] == want
                tok = jnp.broadcast_to(lane + (c * LANES).astype(F32), want.shape)
                aff = jnp.broadcast_to(affc_ref[c, pl.ds(e, 1), :], want.shape)
                return jnp.where(hit, tok, accs[0]), jnp.where(hit, aff, accs[1])

            zeros = jnp.zeros((slot_block, LANES), F32)
            acc_t, acc_g = lax.fori_loop(bounds[pb][0], bounds[pb][1], chunk, (zeros, zeros))
            accs_t.append(acc_t)
            accs_g.append(acc_g)
        tok = _dot_nt(ones, jnp.concatenate(accs_t, axis=0), precision=lax.Precision.HIGHEST)
        idx_ref[e] = tok[0:1, :].astype(jnp.int32)
        gate = jnp.sum(jnp.concatenate(accs_g, axis=0), axis=1, keepdims=True)
        gate_ref[e] = jnp.broadcast_to(gate, (cap, LANES))
        return carry

    lax.fori_loop(0, ne, per_expert, 0)


def _plan(aff, *, cap):
    bsz, ne, seq = aff.shape
    nch = seq // CUMSUM_CHUNK
    r = np.arange(CUMSUM_CHUNK)
    tri = jnp.asarray(r[:, None] <= r[None, :], BF16)
    q = np.arange(nch * ne)
    pfx = jnp.asarray((q[:, None] % ne == q[None, :] % ne) & (q[None, :] // ne < q[:, None] // ne), BF16)
    slot_block = min(cap, LANES)
    return pl.pallas_call(
        functools.partial(_plan_kernel, cap=cap, slot_block=slot_block),
        grid=(bsz,),
        in_specs=[
            pl.BlockSpec((None, ne, seq), lambda b: (b, 0, 0)),
            pl.BlockSpec((CUMSUM_CHUNK, CUMSUM_CHUNK), lambda b: (0, 0)),
            pl.BlockSpec((nch * ne, nch * ne), lambda b: (0, 0)),
        ],
        out_specs=[
            pl.BlockSpec((None, ne, 1, cap), lambda b: (b, 0, 0, 0)),
            pl.BlockSpec((None, ne, cap, LANES), lambda b: (b, 0, 0, 0)),
        ],
        out_shape=[
            jax.ShapeDtypeStruct((bsz, ne, 1, cap), jnp.int32),
            jax.ShapeDtypeStruct((bsz, ne, cap, LANES), F32),
        ],
        scratch_shapes=[
            pltpu.VMEM((seq // LANES, ne, LANES), F32),
            pltpu.VMEM((seq // LANES, ne, LANES), F32),
            pltpu.VMEM((ne, LANES), F32),
        ],
        compiler_params=_cparams("parallel"),
        name="route_plan",
    )(aff, tri, pfx)


ROW_UNROLL = 8
TILES_PER_ITER = 4


def _gather_kernel(idx_ref, h_ref, g_ref, xg_ref, rows_ref):
    b, e = pl.program_id(0), pl.program_id(1)
    cap, d = xg_ref.shape
    base = (b * pl.num_programs(1) + e) * cap

    def move(i, carry):
        for u in range(TILES_PER_ITER):
            t = i * TILES_PER_ITER + u
            for r in range(ROW_UNROLL):
                rows_ref[t, r:r + 1, :] = h_ref[pl.ds(idx_ref[base + t * ROW_UNROLL + r], 1), :]
        return carry

    lax.fori_loop(0, cap // (ROW_UNROLL * TILES_PER_ITER), move, 0)
    xg_ref[...] = _rms(rows_ref[...].reshape(cap, d), g_ref[...]).astype(BF16)


def _gather(idx_flat, h, g, *, ne, cap):
    bsz, seq, d = h.shape
    return pl.pallas_call(
        _gather_kernel,
        grid_spec=pltpu.PrefetchScalarGridSpec(
            num_scalar_prefetch=1,
            grid=(bsz, ne),
            in_specs=[
                pl.BlockSpec((None, seq, d), lambda b, e, idx: (b, 0, 0)),
                pl.BlockSpec((1, d), lambda b, e, idx: (0, 0)),
            ],
            out_specs=pl.BlockSpec((None, None, cap, d), lambda b, e, idx: (e, b, 0, 0)),
            scratch_shapes=[pltpu.VMEM((cap // ROW_UNROLL, ROW_UNROLL, d), F32)],
        ),
        out_shape=jax.ShapeDtypeStruct((ne, bsz, cap, d), BF16),
        compiler_params=_cparams("parallel", "arbitrary"),
        name="moe_gather",
    )(idx_flat, h, g.reshape(1, d))


def _ffn_kernel(x_ref, gate_ref, wg_ref, wu_ref, wd_ref, y_ref, wgb_ref, wub_ref, wdb_ref, *, sub):
    f = pl.program_id(2)
    nf = pl.num_programs(2)

    @pl.when(f == 0)
    def _():
        y_ref[...] = jnp.zeros(y_ref.shape, F32)

    wgb_ref[...] = wg_ref[...].astype(BF16)
    wub_ref[...] = wu_ref[...].astype(BF16)
    wdb_ref[...] = wd_ref[...].astype(BF16)

    nb, cap = gate_ref.shape[0], gate_ref.shape[1]
    blocks = [(bi, s) for bi in range(nb) for s in range(cap // sub)]
    ups = {}
    for i in range(len(blocks) + 1):
        if i < len(blocks):
            bi, s = blocks[i]
            r0 = bi * cap + s * sub
            x = x_ref[r0:r0 + sub, :]
            ups[i] = (_dot(x, wgb_ref[...]), _dot(x, wub_ref[...]))
        if i >= 1:
            bi, s = blocks[i - 1]
            r0 = bi * cap + s * sub
            h1, h2 = ups.pop(i - 1)
            act = (h1 * jax.nn.sigmoid(h1) * h2).astype(BF16)
            y = y_ref[r0:r0 + sub, :] + _dot(act, wdb_ref[...])
            gate = gate_ref[bi, s * sub:(s + 1) * sub, 0:1]
            y_ref[r0:r0 + sub, :] = jnp.where(f == nf - 1, y * gate, y)


def _ffn(xg, gateb, w_gate, w_up, w_down, *, layer, rows, fchunk, sub):
    ne, bsz, cap, d = xg.shape
    m = bsz * cap
    dff = w_gate.shape[3]
    nb = rows // cap
    x2 = xg.reshape(ne, m, d)
    return pl.pallas_call(
        functools.partial(_ffn_kernel, sub=sub),
        grid=(ne, m // rows, dff // fchunk),
        in_specs=[
            pl.BlockSpec((None, rows, d), lambda e, r, f: (e, r, 0)),
            pl.BlockSpec((nb, None, cap, LANES), lambda e, r, f: (r, e, 0, 0)),
            pl.BlockSpec((None, None, d, fchunk), lambda e, r, f: (layer, e, 0, f)),
            pl.BlockSpec((None, None, d, fchunk), lambda e, r, f: (layer, e, 0, f)),
            pl.BlockSpec((None, None, fchunk, d), lambda e, r, f: (layer, e, f, 0)),
        ],
        out_specs=pl.BlockSpec((None, rows, d), lambda e, r, f: (e, r, 0)),
        out_shape=jax.ShapeDtypeStruct((ne, m, d), F32),
        scratch_shapes=[
            pltpu.VMEM((d, fchunk), BF16),
            pltpu.VMEM((d, fchunk), BF16),
            pltpu.VMEM((fchunk, d), BF16),
        ],
        compiler_params=_cparams("parallel", "parallel", "arbitrary"),
        name="moe_experts",
    )(x2, gateb, w_gate, w_up, w_down)


def _scatter_kernel(idx_ref, h_ref, y_ref, g_ref, o_ref, *, final_norm, norm_rows):
    b, e = pl.program_id(0), pl.program_id(1)
    ne = pl.num_programs(1)
    cap = y_ref.shape[0] * ROW_UNROLL
    base = (b * ne + e) * cap
    part = h_ref.shape[0]

    @pl.when(e == 0)
    def _():
        o_ref[...] = jnp.zeros(o_ref.shape, F32)

    r_lo = pl.multiple_of(e * part, part)
    o_ref[pl.ds(r_lo, part), :] = o_ref[pl.ds(r_lo, part), :] + h_ref[...]

    def add(i, carry):
        for u in range(TILES_PER_ITER):
            t = i * TILES_PER_ITER + u
            toks = [idx_ref[base + t * ROW_UNROLL + r] for r in range(ROW_UNROLL)]
            vals = [o_ref[pl.ds(toks[r], 1), :] + y_ref[t, r:r + 1, :] for r in range(ROW_UNROLL)]
            for r in range(ROW_UNROLL):
                o_ref[pl.ds(toks[r], 1), :] = vals[r]
        return carry

    lax.fori_loop(0, cap // (ROW_UNROLL * TILES_PER_ITER), add, 0)

    if final_norm:
        @pl.when(e == ne - 1)
        def _():
            g = g_ref[...]

            def norm(i, carry):
                r0 = pl.multiple_of(i * norm_rows, norm_rows)
                o_ref[pl.ds(r0, norm_rows), :] = _rms(o_ref[pl.ds(r0, norm_rows), :], g)
                return carry

            lax.fori_loop(0, o_ref.shape[0] // norm_rows, norm, 0)


def _scatter(idx_flat, h, y, g_final, *, ne, cap, final_norm):
    bsz, seq, d = h.shape
    y5 = y.reshape(ne, bsz, cap // ROW_UNROLL, ROW_UNROLL, d)
    part = seq // ne
    return pl.pallas_call(
        functools.partial(_scatter_kernel, final_norm=final_norm, norm_rows=min(seq, 256)),
        grid_spec=pltpu.PrefetchScalarGridSpec(
            num_scalar_prefetch=1,
            grid=(bsz, ne),
            in_specs=[
                pl.BlockSpec((None, part, d), lambda b, e, idx: (b, e, 0)),
                pl.BlockSpec((None, None, cap // ROW_UNROLL, ROW_UNROLL, d), lambda b, e, idx: (e, b, 0, 0, 0)),
                pl.BlockSpec((1, d), lambda b, e, idx: (0, 0)),
            ],
            out_specs=pl.BlockSpec((None, seq, d), lambda b, e, idx: (b, 0, 0)),
        ),
        out_shape=jax.ShapeDtypeStruct((bsz, seq, d), F32),
        compiler_params=_cparams("parallel", "arbitrary"),
        name="moe_scatter",
    )(idx_flat, h, y5, g_final.reshape(1, d))


def _moe_layer(h, aff, g, w_gate, w_up, w_down, g_final, *, layer, final_norm, tiles):
    bsz, seq, d = h.shape
    ne = aff.shape[1]
    cap = CAPACITY_FACTOR * seq // ne
    idx, gateb = _plan(aff, cap=cap)
    idx_flat = idx.reshape(-1)
    xg = _gather(idx_flat, h, g, ne=ne, cap=cap)
    y = _ffn(xg, gateb, w_gate, w_up, w_down, layer=layer, rows=tiles["ffn_rows"], fchunk=tiles["ffn_f"],
             sub=tiles["ffn_sub"])
    return _scatter(idx_flat, h, y, g_final, ne=ne, cap=cap, final_norm=final_norm)


def _tiles(bsz, seq, d, dff, cap):
    m = bsz * cap
    rows = min(m, 2048)
    rows = max(cap, rows // cap * cap)
    return {
        "conv": min(seq, 512),
        "qkv": min(seq, 512),
        "tq": min(seq, 512),
        "tk": min(seq, 512),
        "attn_rows": min(seq, 256),
        "oproj": min(seq, 1024),
        "ffn_rows": rows,
        "ffn_f": min(dff, 512),
        "ffn_sub": min(cap, 512),
    }


def kernel(x, norm_mix, norm_ffn, conv_in, conv_w, conv_out, attn_qkv, attn_q_norm, attn_k_norm, attn_out,
           router, w_gate, w_up, w_down, final_norm):
    bsz, seq, d = x.shape
    depth = norm_mix.shape[0]
    ne = router.shape[2]
    tiles = _tiles(bsz, seq, d, w_gate.shape[3], CAPACITY_FACTOR * seq // ne)
    h = x
    for i in range(depth):
        j = i // 2
        if i % 2 == 0:
            h, aff = _conv_layer(h, norm_mix[i], conv_in[j], conv_w[j], conv_out[j], norm_ffn[i], router[i],
                                 tile=tiles["conv"])
        else:
            hd = attn_q_norm.shape[1]
            q, k2, vt = _qkv_layer(h, norm_mix[i], attn_qkv[j], attn_q_norm[j], attn_k_norm[j], tile=tiles["qkv"])
            o = _attention(q, k2, vt, hd=hd, tq=tiles["tq"], rows=tiles["attn_rows"], tk=tiles["tk"])
            h, aff = _oproj_layer(o, h, attn_out[j], norm_ffn[i], router[i], tile=tiles["oproj"])
        h = _moe_layer(h, aff, norm_ffn[i], w_gate, w_up, w_down, final_norm,
                       layer=i, final_norm=(i == depth - 1), tiles=tiles)
    if depth == 0:
        raise NotImplementedError("depth 0")
    return h
```

```python
import functools

import jax
import jax.numpy as jnp
import numpy as np
from jax import lax
from jax.experimental import pallas as pl
from jax.experimental.pallas import tpu as pltpu

EPS = 1e-6
N_HEADS = 16
N_KV_HEADS = 4
GRID_W = 64
ROPE_THETA = 10000.0
CAPACITY_FACTOR = 2
LOG2E = 1.4426950408889634
ATTN_LOOKAHEAD = 3
ONES_ROWS = 16

LANES = 128
CUMSUM_CHUNK = 256
VMEM_LIMIT = 56 * 1024 * 1024

F32 = jnp.float32
BF16 = jnp.bfloat16


def _cparams(*sem):
    return pltpu.CompilerParams(dimension_semantics=sem, vmem_limit_bytes=VMEM_LIMIT)


def _rms(x, g):
    return x * lax.rsqrt(jnp.mean(x * x, axis=-1, keepdims=True) + EPS) * g


def _dot(a, b):
    return jnp.dot(a, b, preferred_element_type=F32)


def _dot_nt(a, b, precision=None):
    return lax.dot_general(a, b, (((1,), (1,)), ((), ())), precision=precision,
                           preferred_element_type=F32)


def _affinities(h, g, w_split):
    ne = w_split.shape[0] // 2
    xn = _rms(h, g)
    xh = xn.astype(BF16)
    xl = (xn - xh.astype(F32)).astype(BF16)
    part = _dot_nt(w_split, xh) + _dot_nt(w_split, xl)
    lg = part[0:ne, :] + part[ne:2 * ne, :]
    ex = jnp.exp(lg - jnp.max(lg, axis=0, keepdims=True))
    return ex / jnp.sum(ex, axis=0, keepdims=True)


def _router_operands(g, w_router):
    wt = w_router.T
    wt_hi = wt.astype(BF16)
    wt_lo = (wt - wt_hi.astype(F32)).astype(BF16)
    return g.reshape(1, -1), jnp.concatenate([wt_hi, wt_lo], axis=0)


def _router_specs(ne, d):
    return [pl.BlockSpec((1, d), lambda b, i: (0, 0)),
            pl.BlockSpec((2 * ne, d), lambda b, i: (0, 0))]


HALO = 8


def _conv_kernel(xp_ref, x_ref, xq_ref, g_ref, win_ref, wc_ref, wout_ref, gr_ref, wr_ref, o_ref, aff_ref, u_ref):
    i = pl.program_id(1)
    n = pl.num_programs(1)
    t = x_ref.shape[0]
    d = x_ref.shape[1]
    x = x_ref[...]
    g = g_ref[...]
    xs = jnp.concatenate([_rms(xp_ref[...], g), _rms(x, g), _rms(xq_ref[...], g)], axis=0).astype(BF16)
    c = _dot(xs, win_ref[:, d:2 * d])
    h = _dot(xs, win_ref[:, 2 * d:3 * d])
    u = c * h
    row = lax.broadcasted_iota(jnp.int32, (t + 2 * HALO, 1), 0)
    first_live = jnp.where(i == 0, HALO, 0)
    end_live = jnp.where(i == n - 1, HALO + t, t + 2 * HALO)
    u_ref[...] = jnp.where((row >= first_live) & (row < end_live), u, 0.0)
    wc = wc_ref[...]
    y = (wc[0:1, :] * u_ref[HALO - 1:HALO - 1 + t, :] + wc[1:2, :] * u_ref[HALO:HALO + t, :]
         + wc[2:3, :] * u_ref[HALO + 1:HALO + 1 + t, :])
    b = _dot(xs, win_ref[:, 0:d])[HALO:HALO + t, :]
    out = x + _dot((b * y).astype(BF16), wout_ref[...])
    o_ref[...] = out
    aff_ref[...] = _affinities(out, gr_ref[...], wr_ref[...])


def _conv_layer(h, g, w_in, w_conv, w_out, g_ffn, w_router, *, tile):
    bsz, seq, d = h.shape
    ne = w_router.shape[1]
    tb = tile // HALO
    nb = seq // HALO
    return pl.pallas_call(
        _conv_kernel,
        grid=(bsz, seq // tile),
        in_specs=[
            pl.BlockSpec((None, HALO, d), lambda b, i: (b, jnp.maximum(i * tb - 1, 0), 0)),
            pl.BlockSpec((None, tile, d), lambda b, i: (b, i, 0)),
            pl.BlockSpec((None, HALO, d), lambda b, i: (b, jnp.minimum((i + 1) * tb, nb - 1), 0)),
            pl.BlockSpec((1, d), lambda b, i: (0, 0)),
            pl.BlockSpec((d, 3 * d), lambda b, i: (0, 0)),
            pl.BlockSpec((3, d), lambda b, i: (0, 0)),
            pl.BlockSpec((d, d), lambda b, i: (0, 0)),
        ] + _router_specs(ne, d),
        out_specs=[
            pl.BlockSpec((None, tile, d), lambda b, i: (b, i, 0)),
            pl.BlockSpec((None, ne, tile), lambda b, i: (b, 0, i)),
        ],
        out_shape=[
            jax.ShapeDtypeStruct((bsz, seq, d), F32),
            jax.ShapeDtypeStruct((bsz, ne, seq), F32),
        ],
        scratch_shapes=[pltpu.VMEM((tile + 2 * HALO, d), F32)],
        compiler_params=_cparams("parallel", "arbitrary"),
        name="conv_mixer",
    )(h, h, h, g.reshape(1, d), w_in.astype(BF16), w_conv, w_out.astype(BF16), *_router_operands(g_ffn, w_router))


def _rope_tables(seq, head_dim, g, scale):
    axis = head_dim // 2
    quarter = axis // 2
    rows = seq // GRID_W
    row = jnp.repeat(jnp.arange(rows, dtype=F32), GRID_W)
    col = jnp.tile(jnp.arange(GRID_W, dtype=F32), rows)
    inv_freq = ROPE_THETA ** (-jnp.arange(0, axis, 2, dtype=F32) / axis)
    j = np.arange(head_dim)
    pos = jnp.where((j < axis)[None, :], row[:, None], col[:, None])
    ang = pos * inv_freq[j % quarter][None, :]
    first = (j % axis) < quarter
    partner = np.where(first, j + quarter, j - quarter)
    cos_t = jnp.cos(ang) * g[None, :] * scale
    sin_t = jnp.sin(ang) * jnp.where(first, -1.0, 1.0)[None, :] * g[partner][None, :] * scale
    return jnp.tile(cos_t, (1, 2)), jnp.tile(sin_t, (1, 2)), partner


def _qkv_kernel(x_ref, g_ref, w_ref, wvt_ref, cq_ref, sq_ref, ck_ref, sk_ref, q_ref, k_ref, vt_ref, *, hd):
    t, d = x_ref.shape
    nq = q_ref.shape[1]
    nk = k_ref.shape[1] // 2
    xn = _rms(x_ref[...], g_ref[...]).astype(BF16)
    lane = lax.broadcasted_iota(jnp.int32, (1, LANES), 1)
    lo = lane < hd

    def head_scale(raw):
        sq = raw * raw
        s0 = jnp.sum(jnp.where(lo, sq, 0.0), axis=-1, keepdims=True)
        s1 = jnp.sum(jnp.where(lo, 0.0, sq), axis=-1, keepdims=True)
        return jnp.where(lo, lax.rsqrt(s0 / hd + EPS), lax.rsqrt(s1 / hd + EPS))

    def dup(pair):
        sw = pltpu.roll(pair, hd, axis=1)
        return jnp.where(lo, pair, sw), jnp.where(lo, sw, pair)

    cq, sq_t = cq_ref[...], sq_ref[...]
    raw_q = _dot(xn, w_ref[:, 0:nq])
    swp_q = _dot(xn, w_ref[:, nq:2 * nq])
    for j in range(nq // LANES):
        raw = raw_q[:, j * LANES:(j + 1) * LANES]
        swp = swp_q[:, j * LANES:(j + 1) * LANES]
        q_ref[:, j * LANES:(j + 1) * LANES] = ((raw * cq + swp * sq_t) * head_scale(raw)).astype(BF16)
    ck, sk_t = ck_ref[...], sk_ref[...]
    raw_k = _dot(xn, w_ref[:, 2 * nq:2 * nq + nk])
    swp_k = _dot(xn, w_ref[:, 2 * nq + nk:2 * nq + 2 * nk])
    for j in range(nk // LANES):
        raw = raw_k[:, j * LANES:(j + 1) * LANES]
        swp = swp_k[:, j * LANES:(j + 1) * LANES]
        ka, kb = dup(((raw * ck + swp * sk_t) * head_scale(raw)).astype(F32))
        k_ref[:, 2 * j * LANES:(2 * j + 1) * LANES] = ka.astype(BF16)
        k_ref[:, (2 * j + 1) * LANES:(2 * j + 2) * LANES] = kb.astype(BF16)
    vt_ref[...] = _dot_nt(wvt_ref[...], xn).astype(BF16)


def _qkv_layer(h, g, w_qkv, g_q, g_k, *, tile):
    bsz, seq, d = h.shape
    hd = g_q.shape[0]
    assert 2 * hd == LANES
    nq, nk = N_HEADS * hd, N_KV_HEADS * hd
    cq, sq, partner = _rope_tables(seq, hd, g_q, hd ** -0.5 * LOG2E)
    ck, sk, _ = _rope_tables(seq, hd, g_k, 1.0)
    wq, wk, wv = w_qkv[:, :nq], w_qkv[:, nq:nq + nk], w_qkv[:, nq + nk:]
    perm_q = (np.arange(nq) // hd) * hd + partner[np.arange(nq) % hd]
    perm_k = (np.arange(nk) // hd) * hd + partner[np.arange(nk) % hd]
    w_all = jnp.concatenate([wq, wq[:, perm_q], wk, wk[:, perm_k]], axis=1).astype(BF16)
    wvt = wv.T.astype(BF16)
    ncol = w_all.shape[1]
    tab = pl.BlockSpec((tile, LANES), lambda b, i: (i, 0))
    return pl.pallas_call(
        functools.partial(_qkv_kernel, hd=hd),
        grid=(bsz, seq // tile),
        in_specs=[
            pl.BlockSpec((None, tile, d), lambda b, i: (b, i, 0)),
            pl.BlockSpec((1, d), lambda b, i: (0, 0)),
            pl.BlockSpec((d, ncol), lambda b, i: (0, 0)),
            pl.BlockSpec((nk, d), lambda b, i: (0, 0)),
            tab, tab, tab, tab,
        ],
        out_specs=[
            pl.BlockSpec((None, tile, nq), lambda b, i: (b, i, 0)),
            pl.BlockSpec((None, tile, 2 * nk), lambda b, i: (b, i, 0)),
            pl.BlockSpec((None, nk, tile), lambda b, i: (b, 0, i)),
        ],
        out_shape=[
            jax.ShapeDtypeStruct((bsz, seq, nq), BF16),
            jax.ShapeDtypeStruct((bsz, seq, 2 * nk), BF16),
            jax.ShapeDtypeStruct((bsz, nk, seq), BF16),
        ],
        compiler_params=_cparams("parallel", "arbitrary"),
        name="qkv_rope",
    )(h, g.reshape(1, d), w_all, wvt, cq, sq, ck, sk)


def _attn_kernel(q_ref, k_ref, vt_ref, o_ref, *, hd, rows, tk):
    nh = q_ref.shape[1] // hd
    tq = q_ref.shape[0]
    seq = k_ref.shape[0]
    lane = lax.broadcasted_iota(jnp.int32, (1, LANES), 1)
    lo = lane < hd
    zero = jnp.zeros((), BF16)
    ones = jnp.ones((ONES_ROWS, tk), BF16)

    for r in range(tq // rows):
        qs = []
        for h in range(nh):
            mine = lo if h % 2 == 0 else jnp.logical_not(lo)
            qs.append(jnp.where(mine, q_ref[r * rows:(r + 1) * rows, (h // 2) * LANES:(h // 2 + 1) * LANES], zero))

        stages = [(kc, h) for kc in range(seq // tk) for h in range(nh)]
        m = [jnp.full((1, rows), -jnp.inf, F32)] * nh
        acc = [jnp.zeros((hd + ONES_ROWS, rows), F32)] * nh
        scores, v_aug = {}, {}
        for i in range(len(stages) + ATTN_LOOKAHEAD):
            if i < len(stages):
                kc, h = stages[i]
                scores[i] = _dot_nt(k_ref[kc * tk:(kc + 1) * tk, :], qs[h])
            j = i - ATTN_LOOKAHEAD
            if j >= 0:
                kc, h = stages[j]
                if kc not in v_aug:
                    v_aug = {kc: jnp.concatenate([vt_ref[:, kc * tk:(kc + 1) * tk], ones], axis=0)}
                s = scores.pop(j)
                m_new = jnp.maximum(m[h], jnp.max(s, axis=0, keepdims=True))
                p = jnp.exp2(s - m_new)
                acc[h] = acc[h] * jnp.exp2(m[h] - m_new) + _dot(v_aug[kc], p.astype(BF16))
                m[h] = m_new
        for pair in range(nh // 2):
            halves = [acc[2 * pair + i][0:hd, :] / acc[2 * pair + i][hd:hd + 1, :] for i in range(2)]
            o_ref[r * rows:(r + 1) * rows, pair * LANES:(pair + 1) * LANES] = (
                jnp.concatenate(halves, axis=0).T.astype(o_ref.dtype))


def _attention(q, k2, vt, *, hd, tq, rows, tk):
    bsz, seq, nq = q.shape
    group = N_HEADS // N_KV_HEADS
    gw = group * hd
    return pl.pallas_call(
        functools.partial(_attn_kernel, hd=hd, rows=rows, tk=tk),
        grid=(bsz, N_KV_HEADS, seq // tq),
        in_specs=[
            pl.BlockSpec((None, tq, gw), lambda b, g, qi: (b, qi, g)),
            pl.BlockSpec((None, seq, LANES), lambda b, g, qi: (b, 0, g)),
            pl.BlockSpec((None, hd, seq), lambda b, g, qi: (b, g, 0)),
        ],
        out_specs=pl.BlockSpec((None, tq, gw), lambda b, g, qi: (b, qi, g)),
        out_shape=jax.ShapeDtypeStruct((bsz, seq, nq), BF16),
        compiler_params=_cparams("parallel", "parallel", "arbitrary"),
        name="flash_attn",
    )(q, k2, vt)


def _oproj_kernel(o_ref, h_ref, w_ref, gr_ref, wr_ref, out_ref, aff_ref):
    out = h_ref[...] + _dot(o_ref[...], w_ref[...])
    out_ref[...] = out
    aff_ref[...] = _affinities(out, gr_ref[...], wr_ref[...])


def _oproj_layer(o, h, w_o, g_ffn, w_router, *, tile):
    bsz, seq, d = h.shape
    nq = o.shape[2]
    ne = w_router.shape[1]
    return pl.pallas_call(
        _oproj_kernel,
        grid=(bsz, seq // tile),
        in_specs=[
            pl.BlockSpec((None, tile, nq), lambda b, i: (b, i, 0)),
            pl.BlockSpec((None, tile, d), lambda b, i: (b, i, 0)),
            pl.BlockSpec((nq, d), lambda b, i: (0, 0)),
        ] + _router_specs(ne, d),
        out_specs=[
            pl.BlockSpec((None, tile, d), lambda b, i: (b, i, 0)),
            pl.BlockSpec((None, ne, tile), lambda b, i: (b, 0, i)),
        ],
        out_shape=[
            jax.ShapeDtypeStruct((bsz, seq, d), F32),
            jax.ShapeDtypeStruct((bsz, ne, seq), F32),
        ],
        compiler_params=_cparams("parallel", "arbitrary"),
        name="attn_out_proj",
    )(o, h, w_o.astype(BF16), *_router_operands(g_ffn, w_router))


def _plan_kernel(aff_ref, tri_ref, pfx_ref, idx_ref, gate_ref, key_ref, affc_ref, filled_ref, *, cap, slot_block):
    ne, seq = aff_ref.shape
    nch = seq // CUMSUM_CHUNK
    a = aff_ref[...]

    def count(mask):
        return jnp.sum(jnp.where(mask, 1.0, 0.0), axis=1, keepdims=True)

    def search(j, thr_bits):
        cand = thr_bits | lax.shift_left(jnp.int32(1), 30 - j)
        return jnp.where(count(a >= pltpu.bitcast(cand, F32)) >= cap, cand, thr_bits)

    thr = pltpu.bitcast(lax.fori_loop(0, 31, search, jnp.zeros((ne, 1), jnp.int32)), F32)

    def cumsum(m):
        st = jnp.concatenate([m[:, c * CUMSUM_CHUNK:(c + 1) * CUMSUM_CHUNK] for c in range(nch)], axis=0)
        within = _dot(st.astype(BF16), tri_ref[...])
        tot = jnp.broadcast_to(within[:, CUMSUM_CHUNK - 1:CUMSUM_CHUNK], (nch * ne, LANES))
        offs = _dot(pfx_ref[...], tot.astype(BF16))
        full = within + offs[:, 0:1]
        return jnp.concatenate([full[c * ne:(c + 1) * ne, :] for c in range(nch)], axis=1)

    gt = a > thr
    eq = a == thr
    need = cap - count(gt)
    eq_rank = cumsum(jnp.where(eq, 1.0, 0.0))
    sel = gt | (eq & (eq_rank <= need))
    rank = cumsum(jnp.where(sel, 1.0, 0.0))
    key = jnp.where(sel, rank, 0.0)

    nck = seq // LANES
    lane_i = lax.broadcasted_iota(jnp.int32, (1, LANES), 1)
    filled = jnp.full((ne, LANES), 1e9, F32)
    for c in range(nck):
        key_ref[c] = key[:, c * LANES:(c + 1) * LANES]
        affc_ref[c] = a[:, c * LANES:(c + 1) * LANES]
        filled = jnp.where(lane_i == c, jnp.max(rank[:, c * LANES:(c + 1) * LANES], axis=1, keepdims=True), filled)
    filled_ref[...] = filled

    nblk = cap // slot_block
    slot1 = (lax.broadcasted_iota(jnp.int32, (slot_block, LANES), 0) + 1).astype(F32)
    lane = lane_i.astype(F32)
    ones = jnp.ones((8, LANES), F32)

    def scalar_count(mask):
        return jnp.sum(jnp.where(mask, 1.0, 0.0), axis=1, keepdims=True).astype(jnp.int32)[0, 0]

    def per_expert(e, carry):
        filled_row = filled_ref[pl.ds(e, 1), :]
        bounds = [(scalar_count(filled_row <= float(pb * slot_block)),
                   jnp.minimum(scalar_count(filled_row < float((pb + 1) * slot_block)) + 1, nck))
                  for pb in range(nblk)]
        accs_t, accs_g = [], []
        for pb in range(nblk):
            want = slot1 + float(pb * slot_block)

            def chunk(c, accs, want=want):
                hit = key_ref[c, pl.ds(e, 1), :] == want
                tok = jnp.broadcast_to(lane + (c * LANES).astype(F32), want.shape)
                aff = jnp.broadcast_to(affc_ref[c, pl.ds(e, 1), :], want.shape)
                return jnp.where(hit, tok, accs[0]), jnp.where(hit, aff, accs[1])

            zeros = jnp.zeros((slot_block, LANES), F32)
            acc_t, acc_g = lax.fori_loop(bounds[pb][0], bounds[pb][1], chunk, (zeros, zeros))
            accs_t.append(acc_t)
            accs_g.append(acc_g)
        tok = _dot_nt(ones, jnp.concatenate(accs_t, axis=0), precision=lax.Precision.HIGHEST)
        idx_ref[e] = tok[0:1, :].astype(jnp.int32)
        gate = jnp.sum(jnp.concatenate(accs_g, axis=0), axis=1, keepdims=True)
        gate_ref[e] = jnp.broadcast_to(gate, (cap, LANES))
        return carry

    lax.fori_loop(0, ne, per_expert, 0)


def _plan(aff, *, cap):
    bsz, ne, seq = aff.shape
    nch = seq // CUMSUM_CHUNK
    r = np.arange(CUMSUM_CHUNK)
    tri = jnp.asarray(r[:, None] <= r[None, :], BF16)
    q = np.arange(nch * ne)
    pfx = jnp.asarray((q[:, None] % ne == q[None, :] % ne) & (q[None, :] // ne < q[:, None] // ne), BF16)
    slot_block = min(cap, LANES)
    return pl.pallas_call(
        functools.partial(_plan_kernel, cap=cap, slot_block=slot_block),
        grid=(bsz,),
        in_specs=[
            pl.BlockSpec((None, ne, seq), lambda b: (b, 0, 0)),
            pl.BlockSpec((CUMSUM_CHUNK, CUMSUM_CHUNK), lambda b: (0, 0)),
            pl.BlockSpec((nch * ne, nch * ne), lambda b: (0, 0)),
        ],
        out_specs=[
            pl.BlockSpec((None, ne, 1, cap), lambda b: (b, 0, 0, 0)),
            pl.BlockSpec((None, ne, cap, LANES), lambda b: (b, 0, 0, 0)),
        ],
        out_shape=[
            jax.ShapeDtypeStruct((bsz, ne, 1, cap), jnp.int32),
            jax.ShapeDtypeStruct((bsz, ne, cap, LANES), F32),
        ],
        scratch_shapes=[
            pltpu.VMEM((seq // LANES, ne, LANES), F32),
            pltpu.VMEM((seq // LANES, ne, LANES), F32),
            pltpu.VMEM((ne, LANES), F32),
        ],
        compiler_params=_cparams("parallel"),
        name="route_plan",
    )(aff, tri, pfx)


ROW_UNROLL = 8
TILES_PER_ITER = 4
GATHER_EXPERTS_PER_STEP = 4
SCATTER_EXPERTS_PER_STEP = 2


def _gather_kernel(idx_ref, h_ref, g_ref, xg_ref, rows_ref):
    b, step = pl.program_id(0), pl.program_id(1)
    nexp, cap, d = xg_ref.shape
    base = (b * pl.num_programs(1) + step) * nexp * cap
    rows = nexp * cap

    def move(i, carry):
        for u in range(TILES_PER_ITER):
            t = i * TILES_PER_ITER + u
            for r in range(ROW_UNROLL):
                rows_ref[t, r:r + 1, :] = h_ref[pl.ds(idx_ref[base + t * ROW_UNROLL + r], 1), :]
        return carry

    lax.fori_loop(0, rows // (ROW_UNROLL * TILES_PER_ITER), move, 0)
    xg_ref[...] = _rms(rows_ref[...].reshape(rows, d), g_ref[...]).astype(BF16).reshape(nexp, cap, d)


def _gather(idx_flat, h, g, *, ne, cap):
    bsz, seq, d = h.shape
    nexp = min(ne, GATHER_EXPERTS_PER_STEP)
    return pl.pallas_call(
        _gather_kernel,
        grid_spec=pltpu.PrefetchScalarGridSpec(
            num_scalar_prefetch=1,
            grid=(bsz, ne // nexp),
            in_specs=[
                pl.BlockSpec((None, seq, d), lambda b, e, idx: (b, 0, 0)),
                pl.BlockSpec((1, d), lambda b, e, idx: (0, 0)),
            ],
            out_specs=pl.BlockSpec((nexp, None, cap, d), lambda b, e, idx: (e, b, 0, 0)),
            scratch_shapes=[pltpu.VMEM((nexp * cap // ROW_UNROLL, ROW_UNROLL, d), F32)],
        ),
        out_shape=jax.ShapeDtypeStruct((ne, bsz, cap, d), BF16),
        compiler_params=_cparams("parallel", "arbitrary"),
        name="moe_gather",
    )(idx_flat, h, g.reshape(1, d))


def _ffn_kernel(x_ref, gate_ref, wg_ref, wu_ref, wd_ref, y_ref, wgb_ref, wub_ref, wdb_ref, *, sub):
    f = pl.program_id(2)
    nf = pl.num_programs(2)

    @pl.when(f == 0)
    def _():
        y_ref[...] = jnp.zeros(y_ref.shape, F32)

    wgb_ref[...] = wg_ref[...].astype(BF16)
    wub_ref[...] = wu_ref[...].astype(BF16)
    wdb_ref[...] = wd_ref[...].astype(BF16)

    nb, cap = gate_ref.shape[0], gate_ref.shape[1]
    blocks = [(bi, s) for bi in range(nb) for s in range(cap // sub)]
    ups = {}
    for i in range(len(blocks) + 1):
        if i < len(blocks):
            bi, s = blocks[i]
            r0 = bi * cap + s * sub
            x = x_ref[r0:r0 + sub, :]
            ups[i] = (_dot(x, wgb_ref[...]), _dot(x, wub_ref[...]))
        if i >= 1:
            bi, s = blocks[i - 1]
            r0 = bi * cap + s * sub
            h1, h2 = ups.pop(i - 1)
            act = (h1 * jax.nn.sigmoid(h1) * h2).astype(BF16)
            y = y_ref[r0:r0 + sub, :] + _dot(act, wdb_ref[...])
            gate = gate_ref[bi, s * sub:(s + 1) * sub, 0:1]
            y_ref[r0:r0 + sub, :] = jnp.where(f == nf - 1, y * gate, y)


def _ffn(xg, gateb, w_gate, w_up, w_down, *, layer, rows, fchunk, sub):
    ne, bsz, cap, d = xg.shape
    m = bsz * cap
    dff = w_gate.shape[3]
    nb = rows // cap
    x2 = xg.reshape(ne, m, d)
    return pl.pallas_call(
        functools.partial(_ffn_kernel, sub=sub),
        grid=(ne, m // rows, dff // fchunk),
        in_specs=[
            pl.BlockSpec((None, rows, d), lambda e, r, f: (e, r, 0)),
            pl.BlockSpec((nb, None, cap, LANES), lambda e, r, f: (r, e, 0, 0)),
            pl.BlockSpec((None, None, d, fchunk), lambda e, r, f: (layer, e, 0, f)),
            pl.BlockSpec((None, None, d, fchunk), lambda e, r, f: (layer, e, 0, f)),
            pl.BlockSpec((None, None, fchunk, d), lambda e, r, f: (layer, e, f, 0)),
        ],
        out_specs=pl.BlockSpec((None, rows, d), lambda e, r, f: (e, r, 0)),
        out_shape=jax.ShapeDtypeStruct((ne, m, d), F32),
        scratch_shapes=[
            pltpu.VMEM((d, fchunk), BF16),
            pltpu.VMEM((d, fchunk), BF16),
            pltpu.VMEM((fchunk, d), BF16),
        ],
        compiler_params=_cparams("parallel", "parallel", "arbitrary"),
        name="moe_experts",
    )(x2, gateb, w_gate, w_up, w_down)


def _scatter_kernel(idx_ref, h_ref, y_ref, g_ref, o_ref, *, final_norm, norm_rows):
    b, e = pl.program_id(0), pl.program_id(1)
    ne = pl.num_programs(1)
    rows = y_ref.shape[0] * y_ref.shape[1] * ROW_UNROLL
    base = (b * ne + e) * rows
    part = h_ref.shape[0]

    @pl.when(e == 0)
    def _():
        o_ref[...] = jnp.zeros(o_ref.shape, F32)

    r_lo = pl.multiple_of(e * part, part)
    o_ref[pl.ds(r_lo, part), :] = o_ref[pl.ds(r_lo, part), :] + h_ref[...]

    def add(i, carry, x):
        for u in range(TILES_PER_ITER):
            t = i * TILES_PER_ITER + u
            toks = [idx_ref[base + x * cap + t * ROW_UNROLL + r] for r in range(ROW_UNROLL)]
            vals = [o_ref[pl.ds(toks[r], 1), :] + y_ref[x, t, r:r + 1, :] for r in range(ROW_UNROLL)]
            for r in range(ROW_UNROLL):
                o_ref[pl.ds(toks[r], 1), :] = vals[r]
        return carry

    cap = y_ref.shape[1] * ROW_UNROLL
    for x in range(y_ref.shape[0]):
        lax.fori_loop(0, cap // (ROW_UNROLL * TILES_PER_ITER), functools.partial(add, x=x), 0)

    if final_norm:
        @pl.when(e == ne - 1)
        def _():
            g = g_ref[...]

            def norm(i, carry):
                r0 = pl.multiple_of(i * norm_rows, norm_rows)
                o_ref[pl.ds(r0, norm_rows), :] = _rms(o_ref[pl.ds(r0, norm_rows), :], g)
                return carry

            lax.fori_loop(0, o_ref.shape[0] // norm_rows, norm, 0)


def _scatter(idx_flat, h, y, g_final, *, ne, cap, final_norm):
    bsz, seq, d = h.shape
    y5 = y.reshape(ne, bsz, cap // ROW_UNROLL, ROW_UNROLL, d)
    nexp = min(ne, SCATTER_EXPERTS_PER_STEP)
    part = seq // (ne // nexp)
    return pl.pallas_call(
        functools.partial(_scatter_kernel, final_norm=final_norm, norm_rows=min(seq, 256)),
        grid_spec=pltpu.PrefetchScalarGridSpec(
            num_scalar_prefetch=1,
            grid=(bsz, ne // nexp),
            in_specs=[
                pl.BlockSpec((None, part, d), lambda b, e, idx: (b, e, 0)),
                pl.BlockSpec((nexp, None, cap // ROW_UNROLL, ROW_UNROLL, d), lambda b, e, idx: (e, b, 0, 0, 0)),
                pl.BlockSpec((1, d), lambda b, e, idx: (0, 0)),
            ],
            out_specs=pl.BlockSpec((None, seq, d), lambda b, e, idx: (b, 0, 0)),
        ),
        out_shape=jax.ShapeDtypeStruct((bsz, seq, d), F32),
        compiler_params=_cparams("parallel", "arbitrary"),
        name="moe_scatter",
    )(idx_flat, h, y5, g_final.reshape(1, d))


def _moe_layer(h, aff, g, w_gate, w_up, w_down, g_final, *, layer, final_norm, tiles):
    bsz, seq, d = h.shape
    ne = aff.shape[1]
    cap = CAPACITY_FACTOR * seq // ne
    idx, gateb = _plan(aff, cap=cap)
    idx_flat = idx.reshape(-1)
    xg = _gather(idx_flat, h, g, ne=ne, cap=cap)
    y = _ffn(xg, gateb, w_gate, w_up, w_down, layer=layer, rows=tiles["ffn_rows"], fchunk=tiles["ffn_f"],
             sub=tiles["ffn_sub"])
    return _scatter(idx_flat, h, y, g_final, ne=ne, cap=cap, final_norm=final_norm)


def _tiles(bsz, seq, d, dff, cap):
    m = bsz * cap
    rows = min(m, 2048)
    rows = max(cap, rows // cap * cap)
    return {
        "conv": min(seq, 512),
        "qkv": min(seq, 512),
        "tq": min(seq, 512),
        "tk": min(seq, 512),
        "attn_rows": min(seq, 256),
        "oproj": min(seq, 1024),
        "ffn_rows": rows,
        "ffn_f": min(dff, 512),
        "ffn_sub": min(cap, 512),
    }


def kernel(x, norm_mix, norm_ffn, conv_in, conv_w, conv_out, attn_qkv, attn_q_norm, attn_k_norm, attn_out,
           router, w_gate, w_up, w_down, final_norm):
    bsz, seq, d = x.shape
    depth = norm_mix.shape[0]
    ne = router.shape[2]
    tiles = _tiles(bsz, seq, d, w_gate.shape[3], CAPACITY_FACTOR * seq // ne)
    h = x
    for i in range(depth):
        j = i // 2
        if i % 2 == 0:
            h, aff = _conv_layer(h, norm_mix[i], conv_in[j], conv_w[j], conv_out[j], norm_ffn[i], router[i],
                                 tile=tiles["conv"])
        else:
            hd = attn_q_norm.shape[1]
            q, k2, vt = _qkv_layer(h, norm_mix[i], attn_qkv[j], attn_q_norm[j], attn_k_norm[j], tile=tiles["qkv"])
            o = _attention(q, k2, vt, hd=hd, tq=tiles["tq"], rows=tiles["attn_rows"], tk=tiles["tk"])
            h, aff = _oproj_layer(o, h, attn_out[j], norm_ffn[i], router[i], tile=tiles["oproj"])
        h = _moe_layer(h, aff, norm_ffn[i], w_gate, w_up, w_down, final_norm,
                       layer=i, final_norm=(i == depth - 1), tiles=tiles)
    if depth == 0:
        raise NotImplementedError("depth 0")
    return h
```

```python
import functools

import jax
import jax.numpy as jnp
import numpy as np
from jax import lax
from jax.experimental import pallas as pl
from jax.experimental.pallas import tpu as pltpu

EPS = 1e-6
N_HEADS = 16
N_KV_HEADS = 4
GRID_W = 64
ROPE_THETA = 10000.0
CAPACITY_FACTOR = 2
LOG2E = 1.4426950408889634
ATTN_LOOKAHEAD = 3
ONES_ROWS = 16

LANES = 128
CUMSUM_CHUNK = 256
VMEM_LIMIT = 56 * 1024 * 1024

F32 = jnp.float32
BF16 = jnp.bfloat16


def _cparams(*sem):
    return pltpu.CompilerParams(dimension_semantics=sem, vmem_limit_bytes=VMEM_LIMIT)


def _rms(x, g):
    return x * lax.rsqrt(jnp.mean(x * x, axis=-1, keepdims=True) + EPS) * g


def _dot(a, b):
    return jnp.dot(a, b, preferred_element_type=F32)


def _dot_nt(a, b, precision=None):
    return lax.dot_general(a, b, (((1,), (1,)), ((), ())), precision=precision,
                           preferred_element_type=F32)


def _affinities(h, g, w_split):
    ne = w_split.shape[0] // 2
    xn = _rms(h, g)
    xh = xn.astype(BF16)
    xl = (xn - xh.astype(F32)).astype(BF16)
    part = _dot_nt(w_split, xh) + _dot_nt(w_split, xl)
    lg = part[0:ne, :] + part[ne:2 * ne, :]
    ex = jnp.exp(lg - jnp.max(lg, axis=0, keepdims=True))
    return ex / jnp.sum(ex, axis=0, keepdims=True)


def _router_operands(g, w_router):
    wt = w_router.T
    wt_hi = wt.astype(BF16)
    wt_lo = (wt - wt_hi.astype(F32)).astype(BF16)
    return g.reshape(1, -1), jnp.concatenate([wt_hi, wt_lo], axis=0)


def _router_specs(ne, d):
    return [pl.BlockSpec((1, d), lambda b, i: (0, 0)),
            pl.BlockSpec((2 * ne, d), lambda b, i: (0, 0))]


HALO = 8


def _conv_kernel(xp_ref, x_ref, xq_ref, g_ref, win_ref, wc_ref, wout_ref, gr_ref, wr_ref, o_ref, aff_ref, u_ref):
    i = pl.program_id(1)
    n = pl.num_programs(1)
    t = x_ref.shape[0]
    d = x_ref.shape[1]
    x = x_ref[...]
    g = g_ref[...]
    xs = jnp.concatenate([_rms(xp_ref[...], g), _rms(x, g), _rms(xq_ref[...], g)], axis=0).astype(BF16)
    c = _dot(xs, win_ref[:, d:2 * d])
    h = _dot(xs, win_ref[:, 2 * d:3 * d])
    u = c * h
    row = lax.broadcasted_iota(jnp.int32, (t + 2 * HALO, 1), 0)
    first_live = jnp.where(i == 0, HALO, 0)
    end_live = jnp.where(i == n - 1, HALO + t, t + 2 * HALO)
    u_ref[...] = jnp.where((row >= first_live) & (row < end_live), u, 0.0)
    wc = wc_ref[...]
    y = (wc[0:1, :] * u_ref[HALO - 1:HALO - 1 + t, :] + wc[1:2, :] * u_ref[HALO:HALO + t, :]
         + wc[2:3, :] * u_ref[HALO + 1:HALO + 1 + t, :])
    b = _dot(xs, win_ref[:, 0:d])[HALO:HALO + t, :]
    out = x + _dot((b * y).astype(BF16), wout_ref[...])
    o_ref[...] = out
    aff_ref[...] = _affinities(out, gr_ref[...], wr_ref[...])


def _conv_layer(h, g, w_in, w_conv, w_out, g_ffn, w_router, *, tile):
    bsz, seq, d = h.shape
    ne = w_router.shape[1]
    tb = tile // HALO
    nb = seq // HALO
    return pl.pallas_call(
        _conv_kernel,
        grid=(bsz, seq // tile),
        in_specs=[
            pl.BlockSpec((None, HALO, d), lambda b, i: (b, jnp.maximum(i * tb - 1, 0), 0)),
            pl.BlockSpec((None, tile, d), lambda b, i: (b, i, 0)),
            pl.BlockSpec((None, HALO, d), lambda b, i: (b, jnp.minimum((i + 1) * tb, nb - 1), 0)),
            pl.BlockSpec((1, d), lambda b, i: (0, 0)),
            pl.BlockSpec((d, 3 * d), lambda b, i: (0, 0)),
            pl.BlockSpec((3, d), lambda b, i: (0, 0)),
            pl.BlockSpec((d, d), lambda b, i: (0, 0)),
        ] + _router_specs(ne, d),
        out_specs=[
            pl.BlockSpec((None, tile, d), lambda b, i: (b, i, 0)),
            pl.BlockSpec((None, ne, tile), lambda b, i: (b, 0, i)),
        ],
        out_shape=[
            jax.ShapeDtypeStruct((bsz, seq, d), F32),
            jax.ShapeDtypeStruct((bsz, ne, seq), F32),
        ],
        scratch_shapes=[pltpu.VMEM((tile + 2 * HALO, d), F32)],
        compiler_params=_cparams("parallel", "arbitrary"),
        name="conv_mixer",
    )(h, h, h, g.reshape(1, d), w_in.astype(BF16), w_conv, w_out.astype(BF16), *_router_operands(g_ffn, w_router))


def _rope_tables(seq, head_dim, g, scale):
    axis = head_dim // 2
    quarter = axis // 2
    rows = seq // GRID_W
    row = jnp.repeat(jnp.arange(rows, dtype=F32), GRID_W)
    col = jnp.tile(jnp.arange(GRID_W, dtype=F32), rows)
    inv_freq = ROPE_THETA ** (-jnp.arange(0, axis, 2, dtype=F32) / axis)
    j = np.arange(head_dim)
    pos = jnp.where((j < axis)[None, :], row[:, None], col[:, None])
    ang = pos * inv_freq[j % quarter][None, :]
    first = (j % axis) < quarter
    partner = np.where(first, j + quarter, j - quarter)
    cos_t = jnp.cos(ang) * g[None, :] * scale
    sin_t = jnp.sin(ang) * jnp.where(first, -1.0, 1.0)[None, :] * g[partner][None, :] * scale
    return jnp.tile(cos_t, (1, 2)), jnp.tile(sin_t, (1, 2)), partner


def _qkv_kernel(x_ref, g_ref, w_ref, wvt_ref, cq_ref, sq_ref, ck_ref, sk_ref, q_ref, k_ref, vt_ref, *, hd):
    t, d = x_ref.shape
    nq = q_ref.shape[1]
    nk = k_ref.shape[1] // 2
    xn = _rms(x_ref[...], g_ref[...]).astype(BF16)
    lane = lax.broadcasted_iota(jnp.int32, (1, LANES), 1)
    lo = lane < hd

    def head_scale(raw):
        sq = raw * raw
        s0 = jnp.sum(jnp.where(lo, sq, 0.0), axis=-1, keepdims=True)
        s1 = jnp.sum(jnp.where(lo, 0.0, sq), axis=-1, keepdims=True)
        return jnp.where(lo, lax.rsqrt(s0 / hd + EPS), lax.rsqrt(s1 / hd + EPS))

    def dup(pair):
        sw = pltpu.roll(pair, hd, axis=1)
        return jnp.where(lo, pair, sw), jnp.where(lo, sw, pair)

    cq, sq_t = cq_ref[...], sq_ref[...]
    raw_q = _dot(xn, w_ref[:, 0:nq])
    swp_q = _dot(xn, w_ref[:, nq:2 * nq])
    for j in range(nq // LANES):
        raw = raw_q[:, j * LANES:(j + 1) * LANES]
        swp = swp_q[:, j * LANES:(j + 1) * LANES]
        q_ref[:, j * LANES:(j + 1) * LANES] = ((raw * cq + swp * sq_t) * head_scale(raw)).astype(BF16)
    ck, sk_t = ck_ref[...], sk_ref[...]
    raw_k = _dot(xn, w_ref[:, 2 * nq:2 * nq + nk])
    swp_k = _dot(xn, w_ref[:, 2 * nq + nk:2 * nq + 2 * nk])
    for j in range(nk // LANES):
        raw = raw_k[:, j * LANES:(j + 1) * LANES]
        swp = swp_k[:, j * LANES:(j + 1) * LANES]
        ka, kb = dup(((raw * ck + swp * sk_t) * head_scale(raw)).astype(F32))
        k_ref[:, 2 * j * LANES:(2 * j + 1) * LANES] = ka.astype(BF16)
        k_ref[:, (2 * j + 1) * LANES:(2 * j + 2) * LANES] = kb.astype(BF16)
    vt_ref[...] = _dot_nt(wvt_ref[...], xn).astype(BF16)


def _qkv_layer(h, g, w_qkv, g_q, g_k, *, tile):
    bsz, seq, d = h.shape
    hd = g_q.shape[0]
    assert 2 * hd == LANES
    nq, nk = N_HEADS * hd, N_KV_HEADS * hd
    cq, sq, partner = _rope_tables(seq, hd, g_q, hd ** -0.5 * LOG2E)
    ck, sk, _ = _rope_tables(seq, hd, g_k, 1.0)
    wq, wk, wv = w_qkv[:, :nq], w_qkv[:, nq:nq + nk], w_qkv[:, nq + nk:]
    perm_q = (np.arange(nq) // hd) * hd + partner[np.arange(nq) % hd]
    perm_k = (np.arange(nk) // hd) * hd + partner[np.arange(nk) % hd]
    w_all = jnp.concatenate([wq, wq[:, perm_q], wk, wk[:, perm_k]], axis=1).astype(BF16)
    wvt = wv.T.astype(BF16)
    ncol = w_all.shape[1]
    tab = pl.BlockSpec((tile, LANES), lambda b, i: (i, 0))
    return pl.pallas_call(
        functools.partial(_qkv_kernel, hd=hd),
        grid=(bsz, seq // tile),
        in_specs=[
            pl.BlockSpec((None, tile, d), lambda b, i: (b, i, 0)),
            pl.BlockSpec((1, d), lambda b, i: (0, 0)),
            pl.BlockSpec((d, ncol), lambda b, i: (0, 0)),
            pl.BlockSpec((nk, d), lambda b, i: (0, 0)),
            tab, tab, tab, tab,
        ],
        out_specs=[
            pl.BlockSpec((None, tile, nq), lambda b, i: (b, i, 0)),
            pl.BlockSpec((None, tile, 2 * nk), lambda b, i: (b, i, 0)),
            pl.BlockSpec((None, nk, tile), lambda b, i: (b, 0, i)),
        ],
        out_shape=[
            jax.ShapeDtypeStruct((bsz, seq, nq), BF16),
            jax.ShapeDtypeStruct((bsz, seq, 2 * nk), BF16),
            jax.ShapeDtypeStruct((bsz, nk, seq), BF16),
        ],
        compiler_params=_cparams("parallel", "arbitrary"),
        name="qkv_rope",
    )(h, g.reshape(1, d), w_all, wvt, cq, sq, ck, sk)


def _attn_kernel(q_ref, k_ref, vt_ref, o_ref, *, hd, rows, tk):
    nh = q_ref.shape[1] // hd
    tq = q_ref.shape[0]
    seq = k_ref.shape[0]
    lane = lax.broadcasted_iota(jnp.int32, (1, LANES), 1)
    lo = lane < hd
    zero = jnp.zeros((), BF16)
    ones = jnp.ones((ONES_ROWS, tk), BF16)

    for r in range(tq // rows):
        qs = []
        for h in range(nh):
            mine = lo if h % 2 == 0 else jnp.logical_not(lo)
            qs.append(jnp.where(mine, q_ref[r * rows:(r + 1) * rows, (h // 2) * LANES:(h // 2 + 1) * LANES], zero))

        stages = [(kc, h) for kc in range(seq // tk) for h in range(nh)]
        m = [jnp.full((1, rows), -jnp.inf, F32)] * nh
        acc = [jnp.zeros((hd + ONES_ROWS, rows), F32)] * nh
        scores, v_aug = {}, {}
        for i in range(len(stages) + ATTN_LOOKAHEAD):
            if i < len(stages):
                kc, h = stages[i]
                scores[i] = _dot_nt(k_ref[kc * tk:(kc + 1) * tk, :], qs[h])
            j = i - ATTN_LOOKAHEAD
            if j >= 0:
                kc, h = stages[j]
                if kc not in v_aug:
                    v_aug = {kc: jnp.concatenate([vt_ref[:, kc * tk:(kc + 1) * tk], ones], axis=0)}
                s = scores.pop(j)
                m_new = jnp.maximum(m[h], jnp.max(s, axis=0, keepdims=True))
                p = jnp.exp2(s - m_new)
                acc[h] = acc[h] * jnp.exp2(m[h] - m_new) + _dot(v_aug[kc], p.astype(BF16))
                m[h] = m_new
        for pair in range(nh // 2):
            halves = [acc[2 * pair + i][0:hd, :] / acc[2 * pair + i][hd:hd + 1, :] for i in range(2)]
            o_ref[r * rows:(r + 1) * rows, pair * LANES:(pair + 1) * LANES] = (
                jnp.concatenate(halves, axis=0).T.astype(o_ref.dtype))


def _attention(q, k2, vt, *, hd, tq, rows, tk):
    bsz, seq, nq = q.shape
    group = N_HEADS // N_KV_HEADS
    gw = group * hd
    return pl.pallas_call(
        functools.partial(_attn_kernel, hd=hd, rows=rows, tk=tk),
        grid=(bsz, N_KV_HEADS, seq // tq),
        in_specs=[
            pl.BlockSpec((None, tq, gw), lambda b, g, qi: (b, qi, g)),
            pl.BlockSpec((None, seq, LANES), lambda b, g, qi: (b, 0, g)),
            pl.BlockSpec((None, hd, seq), lambda b, g, qi: (b, g, 0)),
        ],
        out_specs=pl.BlockSpec((None, tq, gw), lambda b, g, qi: (b, qi, g)),
        out_shape=jax.ShapeDtypeStruct((bsz, seq, nq), BF16),
        compiler_params=_cparams("parallel", "parallel", "arbitrary"),
        name="flash_attn",
    )(q, k2, vt)


def _oproj_kernel(o_ref, h_ref, w_ref, gr_ref, wr_ref, out_ref, aff_ref):
    out = h_ref[...] + _dot(o_ref[...], w_ref[...])
    out_ref[...] = out
    aff_ref[...] = _affinities(out, gr_ref[...], wr_ref[...])


def _oproj_layer(o, h, w_o, g_ffn, w_router, *, tile):
    bsz, seq, d = h.shape
    nq = o.shape[2]
    ne = w_router.shape[1]
    return pl.pallas_call(
        _oproj_kernel,
        grid=(bsz, seq // tile),
        in_specs=[
            pl.BlockSpec((None, tile, nq), lambda b, i: (b, i, 0)),
            pl.BlockSpec((None, tile, d), lambda b, i: (b, i, 0)),
            pl.BlockSpec((nq, d), lambda b, i: (0, 0)),
        ] + _router_specs(ne, d),
        out_specs=[
            pl.BlockSpec((None, tile, d), lambda b, i: (b, i, 0)),
            pl.BlockSpec((None, ne, tile), lambda b, i: (b, 0, i)),
        ],
        out_shape=[
            jax.ShapeDtypeStruct((bsz, seq, d), F32),
            jax.ShapeDtypeStruct((bsz, ne, seq), F32),
        ],
        compiler_params=_cparams("parallel", "arbitrary"),
        name="attn_out_proj",
    )(o, h, w_o.astype(BF16), *_router_operands(g_ffn, w_router))


def _plan_kernel(aff_ref, tri_ref, pfx_ref, idx_ref, gate_ref, key_ref, affc_ref, filled_ref, *, cap, slot_block):
    ne, seq = aff_ref.shape
    nch = seq // CUMSUM_CHUNK
    a = aff_ref[...]

    def count(mask):
        return jnp.sum(jnp.where(mask, 1.0, 0.0), axis=1, keepdims=True)

    def search(j, thr_bits):
        cand = thr_bits | lax.shift_left(jnp.int32(1), 30 - j)
        return jnp.where(count(a >= pltpu.bitcast(cand, F32)) >= cap, cand, thr_bits)

    thr = pltpu.bitcast(lax.fori_loop(0, 31, search, jnp.zeros((ne, 1), jnp.int32)), F32)

    def cumsum(m):
        st = jnp.concatenate([m[:, c * CUMSUM_CHUNK:(c + 1) * CUMSUM_CHUNK] for c in range(nch)], axis=0)
        within = _dot(st.astype(BF16), tri_ref[...])
        tot = jnp.broadcast_to(within[:, CUMSUM_CHUNK - 1:CUMSUM_CHUNK], (nch * ne, LANES))
        offs = _dot(pfx_ref[...], tot.astype(BF16))
        full = within + offs[:, 0:1]
        return jnp.concatenate([full[c * ne:(c + 1) * ne, :] for c in range(nch)], axis=1)

    gt = a > thr
    eq = a == thr
    need = cap - count(gt)
    eq_rank = cumsum(jnp.where(eq, 1.0, 0.0))
    sel = gt | (eq & (eq_rank <= need))
    rank = cumsum(jnp.where(sel, 1.0, 0.0))
    key = jnp.where(sel, rank, 0.0)

    nck = seq // LANES
    lane_i = lax.broadcasted_iota(jnp.int32, (1, LANES), 1)
    filled = jnp.full((ne, LANES), 1e9, F32)
    for c in range(nck):
        key_ref[c] = key[:, c * LANES:(c + 1) * LANES]
        affc_ref[c] = a[:, c * LANES:(c + 1) * LANES]
        filled = jnp.where(lane_i == c, jnp.max(rank[:, c * LANES:(c + 1) * LANES], axis=1, keepdims=True), filled)
    filled_ref[...] = filled

    nblk = cap // slot_block
    slot1 = (lax.broadcasted_iota(jnp.int32, (slot_block, LANES), 0) + 1).astype(F32)
    lane = lane_i.astype(F32)
    ones = jnp.ones((8, LANES), F32)

    def scalar_count(mask):
        return jnp.sum(jnp.where(mask, 1.0, 0.0), axis=1, keepdims=True).astype(jnp.int32)[0, 0]

    def per_expert(e, carry):
        filled_row = filled_ref[pl.ds(e, 1), :]
        bounds = [(scalar_count(filled_row <= float(pb * slot_block)),
                   jnp.minimum(scalar_count(filled_row < float((pb + 1) * slot_block)) + 1, nck))
                  for pb in range(nblk)]
        accs_t, accs_g = [], []
        for pb in range(nblk):
            want = slot1 + float(pb * slot_block)

            def chunk(c, accs, want=want):
                hit = key_ref[c, pl.ds(e, 1), :] == want
                tok = jnp.broadcast_to(lane + (c * LANES).astype(F32), want.shape)
                aff = jnp.broadcast_to(affc_ref[c, pl.ds(e, 1), :], want.shape)
                return jnp.where(hit, tok, accs[0]), jnp.where(hit, aff, accs[1])

            zeros = jnp.zeros((slot_block, LANES), F32)
            acc_t, acc_g = lax.fori_loop(bounds[pb][0], bounds[pb][1], chunk, (zeros, zeros))
            accs_t.append(acc_t)
            accs_g.append(acc_g)
        tok = _dot_nt(ones, jnp.concatenate(accs_t, axis=0), precision=lax.Precision.HIGHEST)
        idx_ref[e] = tok[0:1, :].astype(jnp.int32)
        gate = jnp.sum(jnp.concatenate(accs_g, axis=0), axis=1, keepdims=True)
        gate_ref[e] = jnp.broadcast_to(gate, (cap, LANES))
        return carry

    lax.fori_loop(0, ne, per_expert, 0)


def _plan(aff, *, cap):
    bsz, ne, seq = aff.shape
    nch = seq // CUMSUM_CHUNK
    r = np.arange(CUMSUM_CHUNK)
    tri = jnp.asarray(r[:, None] <= r[None, :], BF16)
    q = np.arange(nch * ne)
    pfx = jnp.asarray((q[:, None] % ne == q[None, :] % ne) & (q[None, :] // ne < q[:, None] // ne), BF16)
    slot_block = min(cap, LANES)
    return pl.pallas_call(
        functools.partial(_plan_kernel, cap=cap, slot_block=slot_block),
        grid=(bsz,),
        in_specs=[
            pl.BlockSpec((None, ne, seq), lambda b: (b, 0, 0)),
            pl.BlockSpec((CUMSUM_CHUNK, CUMSUM_CHUNK), lambda b: (0, 0)),
            pl.BlockSpec((nch * ne, nch * ne), lambda b: (0, 0)),
        ],
        out_specs=[
            pl.BlockSpec((None, ne, 1, cap), lambda b: (b, 0, 0, 0)),
            pl.BlockSpec((None, ne, cap, LANES), lambda b: (b, 0, 0, 0)),
        ],
        out_shape=[
            jax.ShapeDtypeStruct((bsz, ne, 1, cap), jnp.int32),
            jax.ShapeDtypeStruct((bsz, ne, cap, LANES), F32),
        ],
        scratch_shapes=[
            pltpu.VMEM((seq // LANES, ne, LANES), F32),
            pltpu.VMEM((seq // LANES, ne, LANES), F32),
            pltpu.VMEM((ne, LANES), F32),
        ],
        compiler_params=_cparams("parallel"),
        name="route_plan",
    )(aff, tri, pfx)


ROW_UNROLL = 8
TILES_PER_ITER = 4
GATHER_EXPERTS_PER_STEP = 4
SCATTER_EXPERTS_PER_STEP = 2


def _gather_kernel(idx_ref, h_ref, g_ref, xg_ref, rows_ref):
    b, step = pl.program_id(0), pl.program_id(1)
    nexp, cap, d = xg_ref.shape
    base = (b * pl.num_programs(1) + step) * nexp * cap
    rows = nexp * cap

    def move(i, carry):
        for u in range(TILES_PER_ITER):
            t = i * TILES_PER_ITER + u
            for r in range(ROW_UNROLL):
                rows_ref[t, r:r + 1, :] = h_ref[pl.ds(idx_ref[base + t * ROW_UNROLL + r], 1), :]
        return carry

    lax.fori_loop(0, rows // (ROW_UNROLL * TILES_PER_ITER), move, 0)
    xg_ref[...] = _rms(rows_ref[...].reshape(rows, d), g_ref[...]).astype(BF16).reshape(nexp, cap, d)


def _gather(idx_flat, h, g, *, ne, cap):
    bsz, seq, d = h.shape
    nexp = min(ne, GATHER_EXPERTS_PER_STEP)
    return pl.pallas_call(
        _gather_kernel,
        grid_spec=pltpu.PrefetchScalarGridSpec(
            num_scalar_prefetch=1,
            grid=(bsz, ne // nexp),
            in_specs=[
                pl.BlockSpec((None, seq, d), lambda b, e, idx: (b, 0, 0)),
                pl.BlockSpec((1, d), lambda b, e, idx: (0, 0)),
            ],
            out_specs=pl.BlockSpec((nexp, None, cap, d), lambda b, e, idx: (e, b, 0, 0)),
            scratch_shapes=[pltpu.VMEM((nexp * cap // ROW_UNROLL, ROW_UNROLL, d), F32)],
        ),
        out_shape=jax.ShapeDtypeStruct((ne, bsz, cap, d), BF16),
        compiler_params=_cparams("parallel", "arbitrary"),
        name="moe_gather",
    )(idx_flat, h, g.reshape(1, d))


def _ffn_kernel(x_ref, gate_ref, wg_ref, wu_ref, wd_ref, y_ref, wgb_ref, wub_ref, wdb_ref, *, sub):
    f = pl.program_id(2)
    nf = pl.num_programs(2)

    @pl.when(f == 0)
    def _():
        y_ref[...] = jnp.zeros(y_ref.shape, F32)

    wgb_ref[...] = wg_ref[...].astype(BF16)
    wub_ref[...] = wu_ref[...].astype(BF16)
    wdb_ref[...] = wd_ref[...].astype(BF16)

    nb, cap = gate_ref.shape[0], gate_ref.shape[1]
    blocks = [(bi, s) for bi in range(nb) for s in range(cap // sub)]
    ups = {}
    for i in range(len(blocks) + 1):
        if i < len(blocks):
            bi, s = blocks[i]
            r0 = bi * cap + s * sub
            x = x_ref[r0:r0 + sub, :]
            ups[i] = (_dot(x, wgb_ref[...]), _dot(x, wub_ref[...]))
        if i >= 1:
            bi, s = blocks[i - 1]
            r0 = bi * cap + s * sub
            h1, h2 = ups.pop(i - 1)
            act = (h1 * jax.nn.sigmoid(h1) * h2).astype(BF16)
            y = y_ref[r0:r0 + sub, :] + _dot(act, wdb_ref[...])
            gate = gate_ref[bi, s * sub:(s + 1) * sub, 0:1]
            y_ref[r0:r0 + sub, :] = jnp.where(f == nf - 1, y * gate, y)


def _ffn(xg, gateb, w_gate, w_up, w_down, *, layer, rows, fchunk, sub):
    ne, bsz, cap, d = xg.shape
    m = bsz * cap
    dff = w_gate.shape[3]
    nb = rows // cap
    x2 = xg.reshape(ne, m, d)
    return pl.pallas_call(
        functools.partial(_ffn_kernel, sub=sub),
        grid=(ne, m // rows, dff // fchunk),
        in_specs=[
            pl.BlockSpec((None, rows, d), lambda e, r, f: (e, r, 0)),
            pl.BlockSpec((nb, None, cap, LANES), lambda e, r, f: (r, e, 0, 0)),
            pl.BlockSpec((None, None, d, fchunk), lambda e, r, f: (layer, e, 0, f)),
            pl.BlockSpec((None, None, d, fchunk), lambda e, r, f: (layer, e, 0, f)),
            pl.BlockSpec((None, None, fchunk, d), lambda e, r, f: (layer, e, f, 0)),
        ],
        out_specs=pl.BlockSpec((None, rows, d), lambda e, r, f: (e, r, 0)),
        out_shape=jax.ShapeDtypeStruct((ne, m, d), F32),
        scratch_shapes=[
            pltpu.VMEM((d, fchunk), BF16),
            pltpu.VMEM((d, fchunk), BF16),
            pltpu.VMEM((fchunk, d), BF16),
        ],
        compiler_params=_cparams("parallel", "parallel", "arbitrary"),
        name="moe_experts",
    )(x2, gateb, w_gate, w_up, w_down)


def _scatter_kernel(idx_ref, h_ref, y_ref, g_ref, o_ref, *, final_norm, norm_rows):
    b, e = pl.program_id(0), pl.program_id(1)
    ne = pl.num_programs(1)
    rows = y_ref.shape[0] * y_ref.shape[1] * ROW_UNROLL
    base = (b * ne + e) * rows
    part = h_ref.shape[0]

    @pl.when(e == 0)
    def _():
        o_ref[...] = jnp.zeros(o_ref.shape, F32)

    r_lo = pl.multiple_of(e * part, part)
    o_ref[pl.ds(r_lo, part), :] = o_ref[pl.ds(r_lo, part), :] + h_ref[...]

    def add(i, carry, x):
        for u in range(TILES_PER_ITER):
            t = i * TILES_PER_ITER + u
            toks = [idx_ref[base + x * cap + t * ROW_UNROLL + r] for r in range(ROW_UNROLL)]
            vals = [o_ref[pl.ds(toks[r], 1), :] + y_ref[x, t, r:r + 1, :] for r in range(ROW_UNROLL)]
            for r in range(ROW_UNROLL):
                o_ref[pl.ds(toks[r], 1), :] = vals[r]
        return carry

    cap = y_ref.shape[1] * ROW_UNROLL
    for x in range(y_ref.shape[0]):
        lax.fori_loop(0, cap // (ROW_UNROLL * TILES_PER_ITER), functools.partial(add, x=x), 0)

    if final_norm:
        @pl.when(e == ne - 1)
        def _():
            g = g_ref[...]

            def norm(i, carry):
                r0 = pl.multiple_of(i * norm_rows, norm_rows)
                o_ref[pl.ds(r0, norm_rows), :] = _rms(o_ref[pl.ds(r0, norm_rows), :], g)
                return carry

            lax.fori_loop(0, o_ref.shape[0] // norm_rows, norm, 0)


def _scatter(idx_flat, h, y, g_final, *, ne, cap, final_norm):
    bsz, seq, d = h.shape
    y5 = y.reshape(ne, bsz, cap // ROW_UNROLL, ROW_UNROLL, d)
    nexp = min(ne, SCATTER_EXPERTS_PER_STEP)
    part = seq // (ne // nexp)
    return pl.pallas_call(
        functools.partial(_scatter_kernel, final_norm=final_norm, norm_rows=min(seq, 256)),
        grid_spec=pltpu.PrefetchScalarGridSpec(
            num_scalar_prefetch=1,
            grid=(bsz, ne // nexp),
            in_specs=[
                pl.BlockSpec((None, part, d), lambda b, e, idx: (b, e, 0)),
                pl.BlockSpec((nexp, None, cap // ROW_UNROLL, ROW_UNROLL, d), lambda b, e, idx: (e, b, 0, 0, 0)),
                pl.BlockSpec((1, d), lambda b, e, idx: (0, 0)),
            ],
            out_specs=pl.BlockSpec((None, seq, d), lambda b, e, idx: (b, 0, 0)),
        ),
        out_shape=jax.ShapeDtypeStruct((bsz, seq, d), F32),
        compiler_params=_cparams("parallel", "arbitrary"),
        name="moe_scatter",
    )(idx_flat, h, y5, g_final.reshape(1, d))


def _moe_layer(h, aff, g, w_gate, w_up, w_down, g_final, *, layer, final_norm, tiles):
    bsz, seq, d = h.shape
    ne = aff.shape[1]
    cap = CAPACITY_FACTOR * seq // ne
    idx, gateb = _plan(aff, cap=cap)
    idx_flat = idx.reshape(-1)
    xg = _gather(idx_flat, h, g, ne=ne, cap=cap)
    y = _ffn(xg, gateb, w_gate, w_up, w_down, layer=layer, rows=tiles["ffn_rows"], fchunk=tiles["ffn_f"],
             sub=tiles["ffn_sub"])
    return _scatter(idx_flat, h, y, g_final, ne=ne, cap=cap, final_norm=final_norm)


def _tiles(bsz, seq, d, dff, cap):
    m = bsz * cap
    rows = min(m, 2048)
    rows = max(cap, rows // cap * cap)
    return {
        "conv": min(seq, 512),
        "qkv": min(seq, 512),
        "tq": min(seq, 1024),
        "tk": min(seq, 512),
        "attn_rows": min(seq, 256),
        "oproj": min(seq, 1024),
        "ffn_rows": rows,
        "ffn_f": min(dff, 512),
        "ffn_sub": min(cap, 512),
    }


def kernel(x, norm_mix, norm_ffn, conv_in, conv_w, conv_out, attn_qkv, attn_q_norm, attn_k_norm, attn_out,
           router, w_gate, w_up, w_down, final_norm):
    bsz, seq, d = x.shape
    depth = norm_mix.shape[0]
    ne = router.shape[2]
    tiles = _tiles(bsz, seq, d, w_gate.shape[3], CAPACITY_FACTOR * seq // ne)
    h = x
    for i in range(depth):
        j = i // 2
        if i % 2 == 0:
            h, aff = _conv_layer(h, norm_mix[i], conv_in[j], conv_w[j], conv_out[j], norm_ffn[i], router[i],
                                 tile=tiles["conv"])
        else:
            hd = attn_q_norm.shape[1]
            q, k2, vt = _qkv_layer(h, norm_mix[i], attn_qkv[j], attn_q_norm[j], attn_k_norm[j], tile=tiles["qkv"])
            o = _attention(q, k2, vt, hd=hd, tq=tiles["tq"], rows=tiles["attn_rows"], tk=tiles["tk"])
            h, aff = _oproj_layer(o, h, attn_out[j], norm_ffn[i], router[i], tile=tiles["oproj"])
        h = _moe_layer(h, aff, norm_ffn[i], w_gate, w_up, w_down, final_norm,
                       layer=i, final_norm=(i == depth - 1), tiles=tiles)
    if depth == 0:
        raise NotImplementedError("depth 0")
    return h
```

```python
import functools

import jax
import jax.numpy as jnp
import numpy as np
from jax import lax
from jax.experimental import pallas as pl
from jax.experimental.pallas import tpu as pltpu

EPS = 1e-6
N_HEADS = 16
N_KV_HEADS = 4
GRID_W = 64
ROPE_THETA = 10000.0
CAPACITY_FACTOR = 2
LOG2E = 1.4426950408889634
ATTN_LOOKAHEAD = 3
ONES_ROWS = 16

LANES = 128
CUMSUM_CHUNK = 256
VMEM_LIMIT = 56 * 1024 * 1024

F32 = jnp.float32
BF16 = jnp.bfloat16


def _cparams(*sem):
    return pltpu.CompilerParams(dimension_semantics=sem, vmem_limit_bytes=VMEM_LIMIT)


def _rms(x, g):
    return x * lax.rsqrt(jnp.mean(x * x, axis=-1, keepdims=True) + EPS) * g


def _dot(a, b):
    return jnp.dot(a, b, preferred_element_type=F32)


def _dot_nt(a, b, precision=None):
    return lax.dot_general(a, b, (((1,), (1,)), ((), ())), precision=precision,
                           preferred_element_type=F32)


def _affinities(h, g, w_split):
    ne = w_split.shape[0] // 2
    xn = _rms(h, g)
    xh = xn.astype(BF16)
    xl = (xn - xh.astype(F32)).astype(BF16)
    part = _dot_nt(w_split, xh) + _dot_nt(w_split, xl)
    lg = part[0:ne, :] + part[ne:2 * ne, :]
    ex = jnp.exp(lg - jnp.max(lg, axis=0, keepdims=True))
    return ex / jnp.sum(ex, axis=0, keepdims=True)


def _router_operands(g, w_router):
    wt = w_router.T
    wt_hi = wt.astype(BF16)
    wt_lo = (wt - wt_hi.astype(F32)).astype(BF16)
    return g.reshape(1, -1), jnp.concatenate([wt_hi, wt_lo], axis=0)


def _router_specs(ne, d):
    return [pl.BlockSpec((1, d), lambda b, i: (0, 0)),
            pl.BlockSpec((2 * ne, d), lambda b, i: (0, 0))]


HALO = 8


def _conv_kernel(xp_ref, x_ref, xq_ref, g_ref, win_ref, wc_ref, wout_ref, gr_ref, wr_ref, o_ref, aff_ref, u_ref):
    i = pl.program_id(1)
    n = pl.num_programs(1)
    t = x_ref.shape[0]
    d = x_ref.shape[1]
    x = x_ref[...]
    g = g_ref[...]
    xs = jnp.concatenate([_rms(xp_ref[...], g), _rms(x, g), _rms(xq_ref[...], g)], axis=0).astype(BF16)
    c = _dot(xs, win_ref[:, d:2 * d])
    h = _dot(xs, win_ref[:, 2 * d:3 * d])
    u = c * h
    row = lax.broadcasted_iota(jnp.int32, (t + 2 * HALO, 1), 0)
    first_live = jnp.where(i == 0, HALO, 0)
    end_live = jnp.where(i == n - 1, HALO + t, t + 2 * HALO)
    u_ref[...] = jnp.where((row >= first_live) & (row < end_live), u, 0.0)
    wc = wc_ref[...]
    y = (wc[0:1, :] * u_ref[HALO - 1:HALO - 1 + t, :] + wc[1:2, :] * u_ref[HALO:HALO + t, :]
         + wc[2:3, :] * u_ref[HALO + 1:HALO + 1 + t, :])
    b = _dot(xs, win_ref[:, 0:d])[HALO:HALO + t, :]
    out = x + _dot((b * y).astype(BF16), wout_ref[...])
    o_ref[...] = out
    aff_ref[...] = _affinities(out, gr_ref[...], wr_ref[...])


def _conv_layer(h, g, w_in, w_conv, w_out, g_ffn, w_router, *, tile):
    bsz, seq, d = h.shape
    ne = w_router.shape[1]
    tb = tile // HALO
    nb = seq // HALO
    return pl.pallas_call(
        _conv_kernel,
        grid=(bsz, seq // tile),
        in_specs=[
            pl.BlockSpec((None, HALO, d), lambda b, i: (b, jnp.maximum(i * tb - 1, 0), 0)),
            pl.BlockSpec((None, tile, d), lambda b, i: (b, i, 0)),
            pl.BlockSpec((None, HALO, d), lambda b, i: (b, jnp.minimum((i + 1) * tb, nb - 1), 0)),
            pl.BlockSpec((1, d), lambda b, i: (0, 0)),
            pl.BlockSpec((d, 3 * d), lambda b, i: (0, 0)),
            pl.BlockSpec((3, d), lambda b, i: (0, 0)),
            pl.BlockSpec((d, d), lambda b, i: (0, 0)),
        ] + _router_specs(ne, d),
        out_specs=[
            pl.BlockSpec((None, tile, d), lambda b, i: (b, i, 0)),
            pl.BlockSpec((None, ne, tile), lambda b, i: (b, 0, i)),
        ],
        out_shape=[
            jax.ShapeDtypeStruct((bsz, seq, d), F32),
            jax.ShapeDtypeStruct((bsz, ne, seq), F32),
        ],
        scratch_shapes=[pltpu.VMEM((tile + 2 * HALO, d), F32)],
        compiler_params=_cparams("parallel", "arbitrary"),
        name="conv_mixer",
    )(h, h, h, g.reshape(1, d), w_in.astype(BF16), w_conv, w_out.astype(BF16), *_router_operands(g_ffn, w_router))


def _rope_tables(seq, head_dim, g, scale):
    axis = head_dim // 2
    quarter = axis // 2
    rows = seq // GRID_W
    row = jnp.repeat(jnp.arange(rows, dtype=F32), GRID_W)
    col = jnp.tile(jnp.arange(GRID_W, dtype=F32), rows)
    inv_freq = ROPE_THETA ** (-jnp.arange(0, axis, 2, dtype=F32) / axis)
    j = np.arange(head_dim)
    pos = jnp.where((j < axis)[None, :], row[:, None], col[:, None])
    ang = pos * inv_freq[j % quarter][None, :]
    first = (j % axis) < quarter
    partner = np.where(first, j + quarter, j - quarter)
    cos_t = jnp.cos(ang) * g[None, :] * scale
    sin_t = jnp.sin(ang) * jnp.where(first, -1.0, 1.0)[None, :] * g[partner][None, :] * scale
    return jnp.tile(cos_t, (1, 2)), jnp.tile(sin_t, (1, 2)), partner


def _qkv_kernel(x_ref, g_ref, w_ref, wvt_ref, cq_ref, sq_ref, ck_ref, sk_ref, q_ref, k_ref, vt_ref, *, hd):
    t, d = x_ref.shape
    nq = q_ref.shape[1]
    nk = k_ref.shape[1] // 2
    xn = _rms(x_ref[...], g_ref[...]).astype(BF16)
    lane = lax.broadcasted_iota(jnp.int32, (1, LANES), 1)
    lo = lane < hd

    def head_scale(raw):
        sq = raw * raw
        s0 = jnp.sum(jnp.where(lo, sq, 0.0), axis=-1, keepdims=True)
        s1 = jnp.sum(jnp.where(lo, 0.0, sq), axis=-1, keepdims=True)
        return jnp.where(lo, lax.rsqrt(s0 / hd + EPS), lax.rsqrt(s1 / hd + EPS))

    def dup(pair):
        sw = pltpu.roll(pair, hd, axis=1)
        return jnp.where(lo, pair, sw), jnp.where(lo, sw, pair)

    cq, sq_t = cq_ref[...], sq_ref[...]
    raw_q = _dot(xn, w_ref[:, 0:nq])
    swp_q = _dot(xn, w_ref[:, nq:2 * nq])
    for j in range(nq // LANES):
        raw = raw_q[:, j * LANES:(j + 1) * LANES]
        swp = swp_q[:, j * LANES:(j + 1) * LANES]
        q_ref[:, j * LANES:(j + 1) * LANES] = ((raw * cq + swp * sq_t) * head_scale(raw)).astype(BF16)
    ck, sk_t = ck_ref[...], sk_ref[...]
    raw_k = _dot(xn, w_ref[:, 2 * nq:2 * nq + nk])
    swp_k = _dot(xn, w_ref[:, 2 * nq + nk:2 * nq + 2 * nk])
    for j in range(nk // LANES):
        raw = raw_k[:, j * LANES:(j + 1) * LANES]
        swp = swp_k[:, j * LANES:(j + 1) * LANES]
        ka, kb = dup(((raw * ck + swp * sk_t) * head_scale(raw)).astype(F32))
        k_ref[:, 2 * j * LANES:(2 * j + 1) * LANES] = ka.astype(BF16)
        k_ref[:, (2 * j + 1) * LANES:(2 * j + 2) * LANES] = kb.astype(BF16)
    vt_ref[...] = _dot_nt(wvt_ref[...], xn).astype(BF16)


def _qkv_layer(h, g, w_qkv, g_q, g_k, *, tile):
    bsz, seq, d = h.shape
    hd = g_q.shape[0]
    assert 2 * hd == LANES
    nq, nk = N_HEADS * hd, N_KV_HEADS * hd
    cq, sq, partner = _rope_tables(seq, hd, g_q, hd ** -0.5 * LOG2E)
    ck, sk, _ = _rope_tables(seq, hd, g_k, 1.0)
    wq, wk, wv = w_qkv[:, :nq], w_qkv[:, nq:nq + nk], w_qkv[:, nq + nk:]
    perm_q = (np.arange(nq) // hd) * hd + partner[np.arange(nq) % hd]
    perm_k = (np.arange(nk) // hd) * hd + partner[np.arange(nk) % hd]
    w_all = jnp.concatenate([wq, wq[:, perm_q], wk, wk[:, perm_k]], axis=1).astype(BF16)
    wvt = wv.T.astype(BF16)
    ncol = w_all.shape[1]
    tab = pl.BlockSpec((tile, LANES), lambda b, i: (i, 0))
    return pl.pallas_call(
        functools.partial(_qkv_kernel, hd=hd),
        grid=(bsz, seq // tile),
        in_specs=[
            pl.BlockSpec((None, tile, d), lambda b, i: (b, i, 0)),
            pl.BlockSpec((1, d), lambda b, i: (0, 0)),
            pl.BlockSpec((d, ncol), lambda b, i: (0, 0)),
            pl.BlockSpec((nk, d), lambda b, i: (0, 0)),
            tab, tab, tab, tab,
        ],
        out_specs=[
            pl.BlockSpec((None, tile, nq), lambda b, i: (b, i, 0)),
            pl.BlockSpec((None, tile, 2 * nk), lambda b, i: (b, i, 0)),
            pl.BlockSpec((None, nk, tile), lambda b, i: (b, 0, i)),
        ],
        out_shape=[
            jax.ShapeDtypeStruct((bsz, seq, nq), BF16),
            jax.ShapeDtypeStruct((bsz, seq, 2 * nk), BF16),
            jax.ShapeDtypeStruct((bsz, nk, seq), BF16),
        ],
        compiler_params=_cparams("parallel", "arbitrary"),
        name="qkv_rope",
    )(h, g.reshape(1, d), w_all, wvt, cq, sq, ck, sk)


def _attn_kernel(q_ref, k_ref, vt_ref, o_ref, *, hd, rows, tk):
    nh = q_ref.shape[1] // hd
    tq = q_ref.shape[0]
    seq = k_ref.shape[0]
    lane = lax.broadcasted_iota(jnp.int32, (1, LANES), 1)
    lo = lane < hd
    zero = jnp.zeros((), BF16)
    ones = jnp.ones((ONES_ROWS, tk), BF16)

    for r in range(tq // rows):
        qs = []
        for h in range(nh):
            mine = lo if h % 2 == 0 else jnp.logical_not(lo)
            qs.append(jnp.where(mine, q_ref[r * rows:(r + 1) * rows, (h // 2) * LANES:(h // 2 + 1) * LANES], zero))

        stages = [(kc, h) for kc in range(seq // tk) for h in range(nh)]
        m = [jnp.full((1, rows), -jnp.inf, F32)] * nh
        acc = [jnp.zeros((hd + ONES_ROWS, rows), F32)] * nh
        scores, v_aug = {}, {}
        for i in range(len(stages) + ATTN_LOOKAHEAD):
            if i < len(stages):
                kc, h = stages[i]
                scores[i] = _dot_nt(k_ref[kc * tk:(kc + 1) * tk, :], qs[h])
            j = i - ATTN_LOOKAHEAD
            if j >= 0:
                kc, h = stages[j]
                if kc not in v_aug:
                    v_aug = {kc: jnp.concatenate([vt_ref[:, kc * tk:(kc + 1) * tk], ones], axis=0)}
                s = scores.pop(j)
                m_cols, p_cols = [], []
                for c in range(rows // LANES):
                    s_c = s[:, c * LANES:(c + 1) * LANES]
                    m_c = jnp.maximum(m[h][:, c * LANES:(c + 1) * LANES], jnp.max(s_c, axis=0, keepdims=True))
                    m_cols.append(m_c)
                    p_cols.append(jnp.exp2(s_c - m_c).astype(BF16))
                m_new = jnp.concatenate(m_cols, axis=1)
                acc[h] = acc[h] * jnp.exp2(m[h] - m_new) + _dot(v_aug[kc], jnp.concatenate(p_cols, axis=1))
                m[h] = m_new
        for pair in range(nh // 2):
            halves = [acc[2 * pair + i][0:hd, :] / acc[2 * pair + i][hd:hd + 1, :] for i in range(2)]
            o_ref[r * rows:(r + 1) * rows, pair * LANES:(pair + 1) * LANES] = (
                jnp.concatenate(halves, axis=0).T.astype(o_ref.dtype))


def _attention(q, k2, vt, *, hd, tq, rows, tk):
    bsz, seq, nq = q.shape
    group = N_HEADS // N_KV_HEADS
    gw = group * hd
    return pl.pallas_call(
        functools.partial(_attn_kernel, hd=hd, rows=rows, tk=tk),
        grid=(bsz, N_KV_HEADS, seq // tq),
        in_specs=[
            pl.BlockSpec((None, tq, gw), lambda b, g, qi: (b, qi, g)),
            pl.BlockSpec((None, seq, LANES), lambda b, g, qi: (b, 0, g)),
            pl.BlockSpec((None, hd, seq), lambda b, g, qi: (b, g, 0)),
        ],
        out_specs=pl.BlockSpec((None, tq, gw), lambda b, g, qi: (b, qi, g)),
        out_shape=jax.ShapeDtypeStruct((bsz, seq, nq), BF16),
        compiler_params=_cparams("parallel", "parallel", "arbitrary"),
        name="flash_attn",
    )(q, k2, vt)


def _oproj_kernel(o_ref, h_ref, w_ref, gr_ref, wr_ref, out_ref, aff_ref):
    out = h_ref[...] + _dot(o_ref[...], w_ref[...])
    out_ref[...] = out
    aff_ref[...] = _affinities(out, gr_ref[...], wr_ref[...])


def _oproj_layer(o, h, w_o, g_ffn, w_router, *, tile):
    bsz, seq, d = h.shape
    nq = o.shape[2]
    ne = w_router.shape[1]
    return pl.pallas_call(
        _oproj_kernel,
        grid=(bsz, seq // tile),
        in_specs=[
            pl.BlockSpec((None, tile, nq), lambda b, i: (b, i, 0)),
            pl.BlockSpec((None, tile, d), lambda b, i: (b, i, 0)),
            pl.BlockSpec((nq, d), lambda b, i: (0, 0)),
        ] + _router_specs(ne, d),
        out_specs=[
            pl.BlockSpec((None, tile, d), lambda b, i: (b, i, 0)),
            pl.BlockSpec((None, ne, tile), lambda b, i: (b, 0, i)),
        ],
        out_shape=[
            jax.ShapeDtypeStruct((bsz, seq, d), F32),
            jax.ShapeDtypeStruct((bsz, ne, seq), F32),
        ],
        compiler_params=_cparams("parallel", "arbitrary"),
        name="attn_out_proj",
    )(o, h, w_o.astype(BF16), *_router_operands(g_ffn, w_router))


def _plan_kernel(aff_ref, tri_ref, pfx_ref, idx_ref, gate_ref, key_ref, affc_ref, filled_ref, *, cap, slot_block):
    ne, seq = aff_ref.shape
    nch = seq // CUMSUM_CHUNK
    a = aff_ref[...]

    def count(mask):
        return jnp.sum(jnp.where(mask, 1.0, 0.0), axis=1, keepdims=True)

    def search(j, thr_bits):
        cand = thr_bits | lax.shift_left(jnp.int32(1), 30 - j)
        return jnp.where(count(a >= pltpu.bitcast(cand, F32)) >= cap, cand, thr_bits)

    thr = pltpu.bitcast(lax.fori_loop(0, 31, search, jnp.zeros((ne, 1), jnp.int32)), F32)

    def cumsum(m):
        st = jnp.concatenate([m[:, c * CUMSUM_CHUNK:(c + 1) * CUMSUM_CHUNK] for c in range(nch)], axis=0)
        within = _dot(st.astype(BF16), tri_ref[...])
        tot = jnp.broadcast_to(within[:, CUMSUM_CHUNK - 1:CUMSUM_CHUNK], (nch * ne, LANES))
        offs = _dot(pfx_ref[...], tot.astype(BF16))
        full = within + offs[:, 0:1]
        return jnp.concatenate([full[c * ne:(c + 1) * ne, :] for c in range(nch)], axis=1)

    gt = a > thr
    eq = a == thr
    need = cap - count(gt)
    eq_rank = cumsum(jnp.where(eq, 1.0, 0.0))
    sel = gt | (eq & (eq_rank <= need))
    rank = cumsum(jnp.where(sel, 1.0, 0.0))
    key = jnp.where(sel, rank, 0.0)

    nck = seq // LANES
    lane_i = lax.broadcasted_iota(jnp.int32, (1, LANES), 1)
    filled = jnp.full((ne, LANES), 1e9, F32)
    for c in range(nck):
        key_ref[c] = key[:, c * LANES:(c + 1) * LANES]
        affc_ref[c] = a[:, c * LANES:(c + 1) * LANES]
        filled = jnp.where(lane_i == c, jnp.max(rank[:, c * LANES:(c + 1) * LANES], axis=1, keepdims=True), filled)
    filled_ref[...] = filled

    nblk = cap // slot_block
    slot1 = (lax.broadcasted_iota(jnp.int32, (slot_block, LANES), 0) + 1).astype(F32)
    lane = lane_i.astype(F32)
    ones = jnp.ones((8, LANES), F32)

    def scalar_count(mask):
        return jnp.sum(jnp.where(mask, 1.0, 0.0), axis=1, keepdims=True).astype(jnp.int32)[0, 0]

    def per_expert(e, carry):
        filled_row = filled_ref[pl.ds(e, 1), :]
        bounds = [(scalar_count(filled_row <= float(pb * slot_block)),
                   jnp.minimum(scalar_count(filled_row < float((pb + 1) * slot_block)) + 1, nck))
                  for pb in range(nblk)]
        accs_t, accs_g = [], []
        for pb in range(nblk):
            want = slot1 + float(pb * slot_block)

            def chunk(c, accs, want=want):
                hit = key_ref[c, pl.ds(e, 1), :] == want
                tok = jnp.broadcast_to(lane + (c * LANES).astype(F32), want.shape)
                aff = jnp.broadcast_to(affc_ref[c, pl.ds(e, 1), :], want.shape)
                return jnp.where(hit, tok, accs[0]), jnp.where(hit, aff, accs[1])

            zeros = jnp.zeros((slot_block, LANES), F32)
            acc_t, acc_g = lax.fori_loop(bounds[pb][0], bounds[pb][1], chunk, (zeros, zeros))
            accs_t.append(acc_t)
            accs_g.append(acc_g)
        tok = _dot_nt(ones, jnp.concatenate(accs_t, axis=0), precision=lax.Precision.HIGHEST)
        idx_ref[e] = tok[0:1, :].astype(jnp.int32)
        gate = jnp.sum(jnp.concatenate(accs_g, axis=0), axis=1, keepdims=True)
        gate_ref[e] = jnp.broadcast_to(gate, (cap, LANES))
        return carry

    lax.fori_loop(0, ne, per_expert, 0)


def _plan(aff, *, cap):
    bsz, ne, seq = aff.shape
    nch = seq // CUMSUM_CHUNK
    r = np.arange(CUMSUM_CHUNK)
    tri = jnp.asarray(r[:, None] <= r[None, :], BF16)
    q = np.arange(nch * ne)
    pfx = jnp.asarray((q[:, None] % ne == q[None, :] % ne) & (q[None, :] // ne < q[:, None] // ne), BF16)
    slot_block = min(cap, LANES)
    return pl.pallas_call(
        functools.partial(_plan_kernel, cap=cap, slot_block=slot_block),
        grid=(bsz,),
        in_specs=[
            pl.BlockSpec((None, ne, seq), lambda b: (b, 0, 0)),
            pl.BlockSpec((CUMSUM_CHUNK, CUMSUM_CHUNK), lambda b: (0, 0)),
            pl.BlockSpec((nch * ne, nch * ne), lambda b: (0, 0)),
        ],
        out_specs=[
            pl.BlockSpec((None, ne, 1, cap), lambda b: (b, 0, 0, 0)),
            pl.BlockSpec((None, ne, cap, LANES), lambda b: (b, 0, 0, 0)),
        ],
        out_shape=[
            jax.ShapeDtypeStruct((bsz, ne, 1, cap), jnp.int32),
            jax.ShapeDtypeStruct((bsz, ne, cap, LANES), F32),
        ],
        scratch_shapes=[
            pltpu.VMEM((seq // LANES, ne, LANES), F32),
            pltpu.VMEM((seq // LANES, ne, LANES), F32),
            pltpu.VMEM((ne, LANES), F32),
        ],
        compiler_params=_cparams("parallel"),
        name="route_plan",
    )(aff, tri, pfx)


ROW_UNROLL = 8
TILES_PER_ITER = 4
GATHER_EXPERTS_PER_STEP = 4
SCATTER_EXPERTS_PER_STEP = 2


def _gather_kernel(idx_ref, h_ref, g_ref, xg_ref, rows_ref):
    b, step = pl.program_id(0), pl.program_id(1)
    nexp, cap, d = xg_ref.shape
    base = (b * pl.num_programs(1) + step) * nexp * cap
    rows = nexp * cap

    def move(i, carry):
        for u in range(TILES_PER_ITER):
            t = i * TILES_PER_ITER + u
            for r in range(ROW_UNROLL):
                rows_ref[t, r:r + 1, :] = h_ref[pl.ds(idx_ref[base + t * ROW_UNROLL + r], 1), :]
        return carry

    lax.fori_loop(0, rows // (ROW_UNROLL * TILES_PER_ITER), move, 0)
    xg_ref[...] = _rms(rows_ref[...].reshape(rows, d), g_ref[...]).astype(BF16).reshape(nexp, cap, d)


def _gather(idx_flat, h, g, *, ne, cap):
    bsz, seq, d = h.shape
    nexp = min(ne, GATHER_EXPERTS_PER_STEP)
    return pl.pallas_call(
        _gather_kernel,
        grid_spec=pltpu.PrefetchScalarGridSpec(
            num_scalar_prefetch=1,
            grid=(bsz, ne // nexp),
            in_specs=[
                pl.BlockSpec((None, seq, d), lambda b, e, idx: (b, 0, 0)),
                pl.BlockSpec((1, d), lambda b, e, idx: (0, 0)),
            ],
            out_specs=pl.BlockSpec((nexp, None, cap, d), lambda b, e, idx: (e, b, 0, 0)),
            scratch_shapes=[pltpu.VMEM((nexp * cap // ROW_UNROLL, ROW_UNROLL, d), F32)],
        ),
        out_shape=jax.ShapeDtypeStruct((ne, bsz, cap, d), BF16),
        compiler_params=_cparams("parallel", "arbitrary"),
        name="moe_gather",
    )(idx_flat, h, g.reshape(1, d))


def _ffn_kernel(x_ref, gate_ref, wg_ref, wu_ref, wd_ref, y_ref, wgb_ref, wub_ref, wdb_ref, *, sub):
    f = pl.program_id(2)
    nf = pl.num_programs(2)

    @pl.when(f == 0)
    def _():
        y_ref[...] = jnp.zeros(y_ref.shape, F32)

    wgb_ref[...] = wg_ref[...].astype(BF16)
    wub_ref[...] = wu_ref[...].astype(BF16)
    wdb_ref[...] = wd_ref[...].astype(BF16)

    nb, cap = gate_ref.shape[0], gate_ref.shape[1]
    blocks = [(bi, s) for bi in range(nb) for s in range(cap // sub)]
    ups = {}
    for i in range(len(blocks) + 1):
        if i < len(blocks):
            bi, s = blocks[i]
            r0 = bi * cap + s * sub
            x = x_ref[r0:r0 + sub, :]
            ups[i] = (_dot(x, wgb_ref[...]), _dot(x, wub_ref[...]))
        if i >= 1:
            bi, s = blocks[i - 1]
            r0 = bi * cap + s * sub
            h1, h2 = ups.pop(i - 1)
            act = (h1 * jax.nn.sigmoid(h1) * h2).astype(BF16)
            y = y_ref[r0:r0 + sub, :] + _dot(act, wdb_ref[...])
            gate = gate_ref[bi, s * sub:(s + 1) * sub, 0:1]
            y_ref[r0:r0 + sub, :] = jnp.where(f == nf - 1, y * gate, y)


def _ffn(xg, gateb, w_gate, w_up, w_down, *, layer, rows, fchunk, sub):
    ne, bsz, cap, d = xg.shape
    m = bsz * cap
    dff = w_gate.shape[3]
    nb = rows // cap
    x2 = xg.reshape(ne, m, d)
    return pl.pallas_call(
        functools.partial(_ffn_kernel, sub=sub),
        grid=(ne, m // rows, dff // fchunk),
        in_specs=[
            pl.BlockSpec((None, rows, d), lambda e, r, f: (e, r, 0)),
            pl.BlockSpec((nb, None, cap, LANES), lambda e, r, f: (r, e, 0, 0)),
            pl.BlockSpec((None, None, d, fchunk), lambda e, r, f: (layer, e, 0, f)),
            pl.BlockSpec((None, None, d, fchunk), lambda e, r, f: (layer, e, 0, f)),
            pl.BlockSpec((None, None, fchunk, d), lambda e, r, f: (layer, e, f, 0)),
        ],
        out_specs=pl.BlockSpec((None, rows, d), lambda e, r, f: (e, r, 0)),
        out_shape=jax.ShapeDtypeStruct((ne, m, d), F32),
        scratch_shapes=[
            pltpu.VMEM((d, fchunk), BF16),
            pltpu.VMEM((d, fchunk), BF16),
            pltpu.VMEM((fchunk, d), BF16),
        ],
        compiler_params=_cparams("parallel", "parallel", "arbitrary"),
        name="moe_experts",
    )(x2, gateb, w_gate, w_up, w_down)


def _scatter_kernel(idx_ref, h_ref, y_ref, g_ref, o_ref, *, final_norm, norm_rows):
    b, e = pl.program_id(0), pl.program_id(1)
    ne = pl.num_programs(1)
    rows = y_ref.shape[0] * y_ref.shape[1] * ROW_UNROLL
    base = (b * ne + e) * rows
    part = h_ref.shape[0]

    @pl.when(e == 0)
    def _():
        o_ref[...] = jnp.zeros(o_ref.shape, F32)

    r_lo = pl.multiple_of(e * part, part)
    o_ref[pl.ds(r_lo, part), :] = o_ref[pl.ds(r_lo, part), :] + h_ref[...]

    def add(i, carry, x):
        for u in range(TILES_PER_ITER):
            t = i * TILES_PER_ITER + u
            toks = [idx_ref[base + x * cap + t * ROW_UNROLL + r] for r in range(ROW_UNROLL)]
            vals = [o_ref[pl.ds(toks[r], 1), :] + y_ref[x, t, r:r + 1, :] for r in range(ROW_UNROLL)]
            for r in range(ROW_UNROLL):
                o_ref[pl.ds(toks[r], 1), :] = vals[r]
        return carry

    cap = y_ref.shape[1] * ROW_UNROLL
    for x in range(y_ref.shape[0]):
        lax.fori_loop(0, cap // (ROW_UNROLL * TILES_PER_ITER), functools.partial(add, x=x), 0)

    if final_norm:
        @pl.when(e == ne - 1)
        def _():
            g = g_ref[...]

            def norm(i, carry):
                r0 = pl.multiple_of(i * norm_rows, norm_rows)
                o_ref[pl.ds(r0, norm_rows), :] = _rms(o_ref[pl.ds(r0, norm_rows), :], g)
                return carry

            lax.fori_loop(0, o_ref.shape[0] // norm_rows, norm, 0)


def _scatter(idx_flat, h, y, g_final, *, ne, cap, final_norm):
    bsz, seq, d = h.shape
    y5 = y.reshape(ne, bsz, cap // ROW_UNROLL, ROW_UNROLL, d)
    nexp = min(ne, SCATTER_EXPERTS_PER_STEP)
    part = seq // (ne // nexp)
    return pl.pallas_call(
        functools.partial(_scatter_kernel, final_norm=final_norm, norm_rows=min(seq, 256)),
        grid_spec=pltpu.PrefetchScalarGridSpec(
            num_scalar_prefetch=1,
            grid=(bsz, ne // nexp),
            in_specs=[
                pl.BlockSpec((None, part, d), lambda b, e, idx: (b, e, 0)),
                pl.BlockSpec((nexp, None, cap // ROW_UNROLL, ROW_UNROLL, d), lambda b, e, idx: (e, b, 0, 0, 0)),
                pl.BlockSpec((1, d), lambda b, e, idx: (0, 0)),
            ],
            out_specs=pl.BlockSpec((None, seq, d), lambda b, e, idx: (b, 0, 0)),
        ),
        out_shape=jax.ShapeDtypeStruct((bsz, seq, d), F32),
        compiler_params=_cparams("parallel", "arbitrary"),
        name="moe_scatter",
    )(idx_flat, h, y5, g_final.reshape(1, d))


def _moe_layer(h, aff, g, w_gate, w_up, w_down, g_final, *, layer, final_norm, tiles):
    bsz, seq, d = h.shape
    ne = aff.shape[1]
    cap = CAPACITY_FACTOR * seq // ne
    idx, gateb = _plan(aff, cap=cap)
    idx_flat = idx.reshape(-1)
    xg = _gather(idx_flat, h, g, ne=ne, cap=cap)
    y = _ffn(xg, gateb, w_gate, w_up, w_down, layer=layer, rows=tiles["ffn_rows"], fchunk=tiles["ffn_f"],
             sub=tiles["ffn_sub"])
    return _scatter(idx_flat, h, y, g_final, ne=ne, cap=cap, final_norm=final_norm)


def _tiles(bsz, seq, d, dff, cap):
    m = bsz * cap
    rows = min(m, 2048)
    rows = max(cap, rows // cap * cap)
    return {
        "conv": min(seq, 512),
        "qkv": min(seq, 512),
        "tq": min(seq, 1024),
        "tk": min(seq, 512),
        "attn_rows": min(seq, 256),
        "oproj": min(seq, 1024),
        "ffn_rows": rows,
        "ffn_f": min(dff, 512),
        "ffn_sub": min(cap, 512),
    }


def kernel(x, norm_mix, norm_ffn, conv_in, conv_w, conv_out, attn_qkv, attn_q_norm, attn_k_norm, attn_out,
           router, w_gate, w_up, w_down, final_norm):
    bsz, seq, d = x.shape
    depth = norm_mix.shape[0]
    ne = router.shape[2]
    tiles = _tiles(bsz, seq, d, w_gate.shape[3], CAPACITY_FACTOR * seq // ne)
    h = x
    for i in range(depth):
        j = i // 2
        if i % 2 == 0:
            h, aff = _conv_layer(h, norm_mix[i], conv_in[j], conv_w[j], conv_out[j], norm_ffn[i], router[i],
                                 tile=tiles["conv"])
        else:
            hd = attn_q_norm.shape[1]
            q, k2, vt = _qkv_layer(h, norm_mix[i], attn_qkv[j], attn_q_norm[j], attn_k_norm[j], tile=tiles["qkv"])
            o = _attention(q, k2, vt, hd=hd, tq=tiles["tq"], rows=tiles["attn_rows"], tk=tiles["tk"])
            h, aff = _oproj_layer(o, h, attn_out[j], norm_ffn[i], router[i], tile=tiles["oproj"])
        h = _moe_layer(h, aff, norm_ffn[i], w_gate, w_up, w_down, final_norm,
                       layer=i, final_norm=(i == depth - 1), tiles=tiles)
    if depth == 0:
        raise NotImplementedError("depth 0")
    return h
```
